```python
import math
import jax
import jax.numpy as jnp
from jax import lax
import numpy as np

D_MODEL = 2048
BATCH = 8
SEQ = 2048
DEPTH = 2
DEC_BATCH = 128
DEC_SEQ = 4
PAST_LEN = 16384
PAGE_SIZE = 128

N_META = 16
N_A_LAYERS = DEPTH // 2
N_B_LAYERS = DEPTH - N_A_LAYERS
A_HEADS = 8
A_DQK = D_MODEL // (2 * A_HEADS)
A_DV = D_MODEL // A_HEADS
A_QK_W = A_HEADS * A_DQK
A_V_W = A_HEADS * A_DV
A_CHUNK = 64
IGATE_CAP = 15.0
B_HEADS = 16
B_NOPE = 128
B_ROPE = 64
B_VDIM = 128
KV_LORA = 512
Q_LORA = 512
B_V_W = B_HEADS * B_VDIM
B_SCALE = (B_NOPE + B_ROPE) ** -0.5
ROPE_BASE = 10000.0
Q_BLOCK = 128
DN_ALPHA = (2 * DEPTH) ** 0.25
DN_BETA = (8 * DEPTH) ** -0.25
LN_EPS = 1e-5
RMS_EPS = 1e-6

kernel_name = 'yoco_mlstm_mla_decode_step'


def layer_norm(x, g, b):
    xf = x.astype(jnp.float32)
    xc = xf - jnp.mean(xf, -1, keepdims=True)
    var = jnp.mean(xc * xc, -1, keepdims=True)
    y = xc * lax.rsqrt(var + LN_EPS) * g.astype(jnp.float32) + b.astype(jnp.float32)
    return y.astype(x.dtype)


def rms_norm(x, g):
    xf = x.astype(jnp.float32)
    y = xf * lax.rsqrt(jnp.mean(xf * xf, -1, keepdims=True) + RMS_EPS) * g.astype(jnp.float32)
    return y.astype(x.dtype)


def rope(x, pos):
    half = x.shape[-1] // 2
    inv = ROPE_BASE ** (-jnp.arange(half, dtype=jnp.float32) / half)
    ang = pos[:, None] * inv[None, :]
    ang = ang.reshape(ang.shape[:1] + (1,) * (x.ndim - 3) + (half,))
    cos, sin = jnp.cos(ang), jnp.sin(ang)
    xf = x.astype(jnp.float32)
    x1, x2 = xf[..., :half], xf[..., half:]
    return jnp.concatenate([x1 * cos - x2 * sin, x2 * cos + x1 * sin], -1).astype(x.dtype)


def mlstm_project(x, w_in, b_gates):
    bsz, t, _ = x.shape
    u = x @ w_in
    s1 = 2 * A_QK_W
    s2 = s1 + A_V_W
    s3 = s2 + A_V_W
    s4 = s3 + A_V_W
    s5 = s4 + A_HEADS
    q, k, v, o, z, gi, gf = jnp.split(u, [A_QK_W, s1, s2, s3, s4, s5], axis=-1)
    heads = lambda a, d: a.reshape(bsz, t, A_HEADS, d).transpose(0, 2, 1, 3).astype(jnp.float32)
    q = heads(q, A_DQK) * (A_DQK ** -0.5)
    k = heads(k, A_DQK)
    v = heads(v, A_DV)
    gi = (gi + b_gates[:A_HEADS]).astype(jnp.float32)
    gi = (IGATE_CAP * jnp.tanh(gi / IGATE_CAP)).transpose(0, 2, 1)
    lf = jax.nn.log_sigmoid((gf + b_gates[A_HEADS:]).astype(jnp.float32)).transpose(0, 2, 1)
    return (q, k, v, gi, lf), o, z


def mlstm_chunk(state, inp):
    c_prev, n_prev, m_prev = state
    q, k, v, gi, lf = inp
    L = q.shape[2]
    b = jnp.cumsum(lf, axis=-1)
    causal = jnp.tril(jnp.ones((L, L), dtype=bool))
    log_d = jnp.where(causal, b[..., :, None] - b[..., None, :] + gi[..., None, :], -jnp.inf)
    log_inter = b + m_prev[..., None]
    m_t = jnp.maximum(jnp.max(log_d, -1), log_inter)
    w_inter = jnp.exp(log_inter - m_t)
    s = jnp.einsum('bhtd,bhsd->bhts', q, k) * jnp.exp(log_d - m_t[..., None])
    num = w_inter[..., None] * jnp.einsum('bhtd,bhdv->bhtv', q, c_prev) + jnp.einsum('bhts,bhsv->bhtv', s, v)
    den = w_inter * jnp.einsum('bhtd,bhd->bht', q, n_prev) + jnp.sum(s, -1)
    h = num / jnp.maximum(jnp.abs(den), jnp.exp(-m_t))[..., None]
    b_end = b[..., -1]
    log_w = b_end[..., None] - b + gi
    m_new = jnp.maximum(b_end + m_prev, jnp.max(log_w, -1))
    w_c = jnp.exp(b_end + m_prev - m_new)
    w_s = jnp.exp(log_w - m_new[..., None])
    c_new = w_c[..., None, None] * c_prev + jnp.einsum('bhs,bhsd,bhsv->bhdv', w_s, k, v)
    n_new = w_c[..., None] * n_prev + jnp.einsum('bhs,bhsd->bhd', w_s, k)
    return (c_new, n_new, m_new), h


def mlstm_prompt(qkvif):
    q = qkvif[0]
    bsz = q.shape[0]
    state = (jnp.zeros((bsz, A_HEADS, A_DQK, A_DV), jnp.float32),
             jnp.zeros((bsz, A_HEADS, A_DQK), jnp.float32),
             jnp.zeros((bsz, A_HEADS), jnp.float32))
    state, h_meta = mlstm_chunk(state, tuple(a[:, :, :N_META] for a in qkvif))
    n_chunks = (q.shape[2] - N_META) // A_CHUNK

    def to_chunks(a):
        a = a[:, :, N_META:]
        a = a.reshape(a.shape[:2] + (n_chunks, A_CHUNK) + a.shape[3:])
        return jnp.moveaxis(a, 2, 0)

    state, h_rest = lax.scan(mlstm_chunk, state, tuple(to_chunks(a) for a in qkvif))
    h_rest = jnp.moveaxis(h_rest, 0, 2).reshape(bsz, A_HEADS, n_chunks * A_CHUNK, A_DV)
    return jnp.concatenate([h_meta, h_rest], axis=2), state


def mlstm_layer(x, state, w_in, b_gates, norm_g, w_out, ln_g, ln_b):
    qkvif, o, z = mlstm_project(x, w_in, b_gates)
    if state is None:
        h, new_state = mlstm_prompt(qkvif)
    else:
        new_state, h = mlstm_chunk(tuple(s.astype(jnp.float32) for s in state), qkvif)
    bsz, t, _ = x.shape
    h = h.transpose(0, 2, 1, 3)
    h = h * lax.rsqrt(jnp.mean(h * h, -1, keepdims=True) + RMS_EPS) * norm_g.astype(jnp.float32)
    gate = jax.nn.sigmoid(o.astype(jnp.float32)) * jax.nn.silu(z.astype(jnp.float32))
    y = (h.reshape(bsz, t, A_V_W) * gate).astype(x.dtype) @ w_out
    return layer_norm(DN_ALPHA * x + y, ln_g, ln_b), new_state


def mla_shared_kv(h, w_down, norm_g, pos):
    d = h @ w_down
    return rms_norm(d[..., :KV_LORA], norm_g), rope(d[..., KV_LORA:], pos)


def mla_query(x, w_in, q_norm_g, w_qb, pos):
    bsz, t, _ = x.shape
    u = x @ w_in
    q = (rms_norm(u[..., :Q_LORA], q_norm_g) @ w_qb).reshape(bsz, t, B_HEADS, B_NOPE + B_ROPE)
    return q[..., :B_NOPE], rope(q[..., B_NOPE:], pos), u[..., Q_LORA:]


def mla_attend_prompt(q_nope, q_pe, k_nope, v, k_pe):
    bsz, t = q_nope.shape[:2]
    n_blk = -(-t // Q_BLOCK)
    t_pad = n_blk * Q_BLOCK

    def blocks(a):
        a = jnp.pad(a, [(0, 0), (0, t_pad - t)] + [(0, 0)] * (a.ndim - 2))
        return jnp.moveaxis(a.reshape((bsz, n_blk, Q_BLOCK) + a.shape[2:]), 1, 0)

    k_idx = jnp.arange(t)

    def one_block(args):
        qn, qr, start = args
        s = jnp.einsum('bqhd,bkhd->bhqk', qn, k_nope) + jnp.einsum('bqhr,bkr->bhqk', qr, k_pe)
        q_idx = start + jnp.arange(Q_BLOCK)
        s = jnp.where(k_idx[None, :] <= q_idx[:, None], s.astype(jnp.float32) * B_SCALE, -jnp.inf)
        p = jax.nn.softmax(s, axis=-1).astype(v.dtype)
        return jnp.einsum('bhqk,bkhv->bqhv', p, v)

    out = lax.map(one_block, (blocks(q_nope), blocks(q_pe), jnp.arange(n_blk) * Q_BLOCK))
    return jnp.moveaxis(out, 0, 1).reshape(bsz, t_pad, B_HEADS, B_VDIM)[:, :t]


def mla_attend_sample(q_nope, q_pe, c_new, pe_new, cache_ckv, cache_kpe, page_table, w_uk, w_uv):
    n_new = q_nope.shape[1]
    past = page_table.shape[1] * PAGE_SIZE
    q_lat = jnp.einsum('bshn,chn->bshc', q_nope, w_uk)
    k_idx = jnp.arange(past + n_new)
    q_idx = past + jnp.arange(n_new)
    mask = k_idx[None, :] <= q_idx[:, None]

    def one_seq(args):
        pt, ql, qr, cn, pn = args
        c_all = jnp.concatenate([cache_ckv[pt].reshape(-1, KV_LORA), cn.astype(cache_ckv.dtype)], 0)
        pe_all = jnp.concatenate([cache_kpe[pt].reshape(-1, B_ROPE), pn.astype(cache_kpe.dtype)], 0)
        s = jnp.einsum('shc,kc->hsk', ql, c_all) + jnp.einsum('shr,kr->hsk', qr, pe_all)
        s = jnp.where(mask[None], s.astype(jnp.float32) * B_SCALE, -jnp.inf)
        p = jax.nn.softmax(s, axis=-1).astype(c_all.dtype)
        return jnp.einsum('hsk,kc->shc', p, c_all)

    o_lat = lax.map(one_seq, (page_table, q_lat, q_pe, c_new, pe_new))
    return jnp.einsum('bshc,chv->bshv', o_lat, w_uv)


def mla_layer_out(x, o, z, w_out, ln_g, ln_b):
    bsz, t, _ = x.shape
    g = (o.reshape(bsz, t, B_V_W).astype(jnp.float32) * jax.nn.silu(z.astype(jnp.float32))).astype(x.dtype)
    return layer_norm(DN_ALPHA * x + g @ w_out, ln_g, ln_b)


def setup_inputs(seed: int = 0) -> dict:
    key = jax.random.key(seed)
    ks = jax.random.split(key, 26)
    f32 = jnp.float32

    def nrm(k, shape, scale=1.0):
        return scale * jax.random.normal(k, shape, f32)

    n_pages = PAST_LEN // PAGE_SIZE
    n_used = DEC_BATCH * n_pages
    n_phys = n_used + n_used // 4
    page_table = jax.random.permutation(ks[0], n_phys)[:n_used].reshape(DEC_BATCH, n_pages).astype(jnp.int32)
    a_in = 2 * A_QK_W + 3 * A_V_W + 2 * A_HEADS
    a_b_gates = jnp.concatenate(
        [nrm(ks[1], (N_A_LAYERS, A_HEADS), 0.1),
         jnp.linspace(3.0, 6.0, A_HEADS, dtype=f32)[None, :] + nrm(ks[2], (N_A_LAYERS, A_HEADS), 0.1)], axis=-1)
    return dict(
        x_prompt=nrm(ks[3], (BATCH, SEQ, D_MODEL)),
        x_sample=nrm(ks[4], (DEC_BATCH, DEC_SEQ, D_MODEL)),
        state_mlstm_C=nrm(ks[5], (DEC_BATCH, N_A_LAYERS, A_HEADS, A_DQK, A_DV), 0.5),
        state_mlstm_n=nrm(ks[6], (DEC_BATCH, N_A_LAYERS, A_HEADS, A_DQK), 0.5),
        state_mlstm_m=nrm(ks[7], (DEC_BATCH, N_A_LAYERS, A_HEADS)),
        cache_ckv=nrm(ks[8], (n_phys, PAGE_SIZE, KV_LORA)),
        cache_kpe=nrm(ks[9], (n_phys, PAGE_SIZE, B_ROPE)),
        page_table=page_table,
        meta_tokens=nrm(ks[10], (N_META, D_MODEL)),
        a_w_in=nrm(ks[11], (N_A_LAYERS, D_MODEL, a_in), D_MODEL ** -0.5),
        a_b_gates=a_b_gates,
        a_norm_g=1.0 + nrm(ks[12], (N_A_LAYERS, A_HEADS, A_DV), 0.02),
        a_w_out=nrm(ks[13], (N_A_LAYERS, A_V_W, D_MODEL), DN_BETA * A_V_W ** -0.5),
        a_ln_g=1.0 + nrm(ks[14], (N_A_LAYERS, D_MODEL), 0.02),
        a_ln_b=nrm(ks[15], (N_A_LAYERS, D_MODEL), 0.02),
        kv_w_down=nrm(ks[16], (D_MODEL, KV_LORA + B_ROPE), D_MODEL ** -0.5),
        kv_norm_g=1.0 + nrm(ks[17], (KV_LORA,), 0.02),
        kv_w_up=nrm(ks[18], (KV_LORA, B_HEADS * (B_NOPE + B_VDIM)), KV_LORA ** -0.5),
        b_w_in=nrm(ks[19], (N_B_LAYERS, D_MODEL, Q_LORA + B_V_W), D_MODEL ** -0.5),
        b_q_norm_g=1.0 + nrm(ks[20], (N_B_LAYERS, Q_LORA), 0.02),
        b_w_qb=nrm(ks[21], (N_B_LAYERS, Q_LORA, B_HEADS * (B_NOPE + B_ROPE)), Q_LORA ** -0.5),
        b_w_out=nrm(ks[22], (N_B_LAYERS, B_V_W, D_MODEL), DN_BETA * B_V_W ** -0.5),
        b_ln_g=1.0 + nrm(ks[23], (N_B_LAYERS, D_MODEL), 0.02),
        b_ln_b=nrm(ks[24], (N_B_LAYERS, D_MODEL), 0.02),
    )


def reference(x_prompt, x_sample, state_mlstm_C, state_mlstm_n, state_mlstm_m, cache_ckv, cache_kpe,
              page_table, meta_tokens, a_w_in, a_b_gates, a_norm_g, a_w_out, a_ln_g, a_ln_b,
              kv_w_down, kv_norm_g, kv_w_up, b_w_in, b_q_norm_g, b_w_qb, b_w_out, b_ln_g, b_ln_b):
    bsz = x_prompt.shape[0]
    meta = jnp.broadcast_to(meta_tokens.astype(x_prompt.dtype)[None], (bsz, N_META, D_MODEL))
    xp = jnp.concatenate([meta, x_prompt], axis=1)
    xs = x_sample
    past = page_table.shape[1] * PAGE_SIZE
    pos_p = jnp.arange(xp.shape[1], dtype=jnp.float32)
    pos_s = past + jnp.arange(xs.shape[1], dtype=jnp.float32)
    w_up = kv_w_up.reshape(KV_LORA, B_HEADS, B_NOPE + B_VDIM)
    w_uk, w_uv = w_up[..., :B_NOPE], w_up[..., B_NOPE:]
    st_p, st_s = [], []
    for layer in range(DEPTH):
        if layer < N_A_LAYERS:
            i = layer
            w = (a_w_in[i], a_b_gates[i], a_norm_g[i], a_w_out[i], a_ln_g[i], a_ln_b[i])
            xp, sp = mlstm_layer(xp, None, *w)
            xs, ss = mlstm_layer(xs, (state_mlstm_C[:, i], state_mlstm_n[:, i], state_mlstm_m[:, i]), *w)
            st_p.append(sp)
            st_s.append(ss)
        else:
            j = layer - N_A_LAYERS
            if j == 0:
                ckv_p, kpe_p = mla_shared_kv(xp, kv_w_down, kv_norm_g, pos_p)
                ckv_s, kpe_s = mla_shared_kv(xs, kv_w_down, kv_norm_g, pos_s)
                kv_p = jnp.einsum('btc,chd->bthd', ckv_p, w_up)
                k_nope_p, v_p = kv_p[..., :B_NOPE], kv_p[..., B_NOPE:]
            qn, qr, z = mla_query(xp, b_w_in[j], b_q_norm_g[j], b_w_qb[j], pos_p)
            o = mla_attend_prompt(qn, qr, k_nope_p, v_p, kpe_p)
            xp = mla_layer_out(xp, o, z, b_w_out[j], b_ln_g[j], b_ln_b[j])
            qn, qr, z = mla_query(xs, b_w_in[j], b_q_norm_g[j], b_w_qb[j], pos_s)
            o = mla_attend_sample(qn, qr, ckv_s, kpe_s, cache_ckv, cache_kpe, page_table, w_uk, w_uv)
            xs = mla_layer_out(xs, o, z, b_w_out[j], b_ln_g[j], b_ln_b[j])
    c_p = jnp.stack([s[0] for s in st_p], axis=1)
    n_p = jnp.stack([s[1] for s in st_p], axis=1)
    m_p = jnp.stack([s[2] for s in st_p], axis=1)
    c_s = jnp.stack([s[0] for s in st_s], axis=1)
    n_s = jnp.stack([s[1] for s in st_s], axis=1)
    m_s = jnp.stack([s[2] for s in st_s], axis=1)
    return (xp[:, N_META:], xs, c_p, n_p, m_p, ckv_p, kpe_p, c_s, n_s, m_s, ckv_s, kpe_s)
```

```python
import functools

import jax
import jax.numpy as jnp
from jax import lax
from jax.experimental import pallas as pl
from jax.experimental.pallas import tpu as pltpu

F32 = jnp.float32
BF16 = jnp.bfloat16

N_META = 16
A_HEADS = 8
A_CHUNK = 64
IGATE_CAP = 15.0
B_HEADS = 16
B_NOPE = 128
B_ROPE = 64
B_VDIM = 128
KV_LORA = 512
Q_LORA = 512
PAGE_SIZE = 128
ROPE_BASE = 10000.0
LN_EPS = 1e-5
RMS_EPS = 1e-6
B_SCALE = (B_NOPE + B_ROPE) ** -0.5

LANES = 128
BF16_SUBLANES = 16
VMEM_LIMIT_BYTES = 56 * 1024 * 1024
HEAD_GROUP = 4
QK_HEAD_W = 2 * LANES

NT_DIMS = (((1,), (1,)), ((), ()))
TN_DIMS = (((0,), (0,)), ((), ()))


def _cparams(*sem):
    return pltpu.CompilerParams(dimension_semantics=sem, vmem_limit_bytes=VMEM_LIMIT_BYTES)


def _pick_block(m, target, mult=BF16_SUBLANES):
    best = None
    for d in range(mult, min(m, target) + 1, mult):
        if m % d == 0:
            best = d
    return best if best is not None else m


def _layer_norm(y, g, b):
    mu = jnp.mean(y, axis=-1, keepdims=True)
    yc = y - mu
    var = jnp.mean(yc * yc, axis=-1, keepdims=True)
    return yc * lax.rsqrt(var + LN_EPS) * g + b


def _rms_norm(y, g):
    return y * lax.rsqrt(jnp.mean(y * y, axis=-1, keepdims=True) + RMS_EPS) * g


def _rope_dup(v, cs, sn):
    return v * cs + pltpu.roll(v, 2 * (B_ROPE // 2), axis=1) * sn


def _silu(z):
    return z * jax.nn.sigmoid(z)


def _mm_plain_kernel(x_ref, w_ref, o_ref):
    acc = jnp.dot(x_ref[...].astype(BF16), w_ref[...], preferred_element_type=F32)
    o_ref[...] = acc.astype(o_ref.dtype)


def _mm_plain(x, w, out_dtype, *, bm, bn, name):
    m, k = x.shape
    n = w.shape[1]
    return pl.pallas_call(
        _mm_plain_kernel,
        grid=(n // bn, m // bm),
        in_specs=[pl.BlockSpec((bm, k), lambda j, i: (i, 0)),
                  pl.BlockSpec((k, bn), lambda j, i: (0, j))],
        out_specs=pl.BlockSpec((bm, bn), lambda j, i: (i, j)),
        out_shape=jax.ShapeDtypeStruct((m, n), out_dtype),
        compiler_params=_cparams("arbitrary", "arbitrary"),
        name=name,
    )(x, w)


def _mm_gates_kernel(x_ref, w_ref, b_ref, o_ref):
    a = jnp.dot(x_ref[...].astype(BF16), w_ref[...], preferred_element_type=F32) + b_ref[...]
    lane = lax.broadcasted_iota(jnp.int32, a.shape, 1)
    gi = IGATE_CAP * jnp.tanh(a / IGATE_CAP)
    lf = jnp.minimum(a, 0.0) - jnp.log1p(jnp.exp(-jnp.abs(a)))
    o_ref[...] = jnp.where(lane < A_HEADS, gi, lf)


def _mm_gates(x, w, b, *, bm, name):
    m, k = x.shape
    return pl.pallas_call(
        _mm_gates_kernel,
        grid=(m // bm,),
        in_specs=[pl.BlockSpec((bm, k), lambda i: (i, 0)),
                  pl.BlockSpec((k, LANES), lambda i: (0, 0)),
                  pl.BlockSpec((1, LANES), lambda i: (0, 0))],
        out_specs=pl.BlockSpec((bm, LANES), lambda i: (i, 0)),
        out_shape=jax.ShapeDtypeStruct((m, LANES), F32),
        compiler_params=_cparams("arbitrary"),
        name=name,
    )(x, w, b)


def _mm_resid_ln_kernel(x_ref, w_ref, r_ref, g_ref, b_ref, *o_refs, alpha):
    acc = jnp.dot(x_ref[...].astype(BF16), w_ref[...], preferred_element_type=F32)
    y = _layer_norm(alpha * r_ref[...] + acc, g_ref[...], b_ref[...])
    for o_ref in o_refs:
        o_ref[...] = y.astype(o_ref.dtype)


def _mm_resid_ln(x, w, resid, g, b, alpha, out_dtypes, *, bm, name):
    m, k = x.shape
    n = w.shape[1]
    row = lambda i: (i, 0)
    fixed = lambda i: (0, 0)
    outs = pl.pallas_call(
        functools.partial(_mm_resid_ln_kernel, alpha=alpha),
        grid=(m // bm,),
        in_specs=[pl.BlockSpec((bm, k), row), pl.BlockSpec((k, n), fixed),
                  pl.BlockSpec((bm, n), row), pl.BlockSpec((1, n), fixed), pl.BlockSpec((1, n), fixed)],
        out_specs=[pl.BlockSpec((bm, n), row) for _ in out_dtypes],
        out_shape=[jax.ShapeDtypeStruct((m, n), dt) for dt in out_dtypes],
        compiler_params=_cparams("arbitrary"),
        name=name,
    )(x, w, resid, g.reshape(1, n), b.reshape(1, n))
    return outs


def _mm_kv_down_kernel(x_ref, w_ref, g_ref, cs_ref, sn_ref, ckv_ref, kpe_ref, ckv16_ref, kpe16_ref):
    acc = jnp.dot(x_ref[...].astype(BF16), w_ref[...], preferred_element_type=F32)
    ckv = _rms_norm(acc[:, :KV_LORA], g_ref[...])
    pe = _rope_dup(acc[:, KV_LORA:], cs_ref[...], sn_ref[...])
    ckv_ref[...] = ckv
    ckv16_ref[...] = ckv.astype(BF16)
    kpe_ref[...] = pe[:, :B_ROPE]
    kpe16_ref[...] = pe.astype(BF16)


def _mm_kv_down(x, w_ext, g, cs, sn, *, bm, name):
    m, k = x.shape
    n = w_ext.shape[1]
    nt = cs.shape[0] // bm
    row = lambda i: (i, 0)
    fixed = lambda i: (0, 0)
    tab = lambda i: (i % nt, 0)
    return pl.pallas_call(
        _mm_kv_down_kernel,
        grid=(m // bm,),
        in_specs=[pl.BlockSpec((bm, k), row), pl.BlockSpec((k, n), fixed), pl.BlockSpec((1, KV_LORA), fixed),
                  pl.BlockSpec((bm, LANES), tab), pl.BlockSpec((bm, LANES), tab)],
        out_specs=[pl.BlockSpec((bm, KV_LORA), row), pl.BlockSpec((bm, B_ROPE), row),
                   pl.BlockSpec((bm, KV_LORA), row), pl.BlockSpec((bm, LANES), row)],
        out_shape=[jax.ShapeDtypeStruct((m, KV_LORA), F32), jax.ShapeDtypeStruct((m, B_ROPE), F32),
                   jax.ShapeDtypeStruct((m, KV_LORA), BF16), jax.ShapeDtypeStruct((m, LANES), BF16)],
        compiler_params=_cparams("arbitrary"),
        name=name,
    )(x, w_ext, g.reshape(1, KV_LORA), cs, sn)


def _mm_kv_up_kernel(c_ref, w_ref, kpe_ref, k_ref, v_ref):
    acc = jnp.dot(c_ref[...], w_ref[...], preferred_element_type=F32)
    kpe = kpe_ref[...]
    for h in range(HEAD_GROUP):
        k_ref[:, h * QK_HEAD_W:h * QK_HEAD_W + B_NOPE] = acc[:, h * B_NOPE:(h + 1) * B_NOPE].astype(BF16)
        k_ref[:, h * QK_HEAD_W + B_NOPE:(h + 1) * QK_HEAD_W] = kpe
    v_ref[...] = acc[:, HEAD_GROUP * B_NOPE:].astype(BF16)


def _mm_kv_up(ckv16, w_perm, kpe16, *, bm, name):
    m, k = ckv16.shape
    ng = B_HEADS // HEAD_GROUP
    wn = HEAD_GROUP * (B_NOPE + B_VDIM)
    return pl.pallas_call(
        _mm_kv_up_kernel,
        grid=(m // bm, ng),
        in_specs=[pl.BlockSpec((bm, k), lambda i, j: (i, 0)),
                  pl.BlockSpec((k, wn), lambda i, j: (0, j)),
                  pl.BlockSpec((bm, LANES), lambda i, j: (i, 0))],
        out_specs=[pl.BlockSpec((bm, HEAD_GROUP * QK_HEAD_W), lambda i, j: (i, j)),
                   pl.BlockSpec((bm, HEAD_GROUP * B_VDIM), lambda i, j: (i, j))],
        out_shape=[jax.ShapeDtypeStruct((m, B_HEADS * QK_HEAD_W), BF16),
                   jax.ShapeDtypeStruct((m, B_HEADS * B_VDIM), BF16)],
        compiler_params=_cparams("arbitrary", "arbitrary"),
        name=name,
    )(ckv16, w_perm, kpe16)


def _mm_rms_kernel(x_ref, w_ref, g_ref, o_ref):
    acc = jnp.dot(x_ref[...].astype(BF16), w_ref[...], preferred_element_type=F32)
    o_ref[...] = _rms_norm(acc, g_ref[...]).astype(o_ref.dtype)


def _mm_rms(x, w, g, *, bm, name):
    m, k = x.shape
    n = w.shape[1]
    return pl.pallas_call(
        _mm_rms_kernel,
        grid=(m // bm,),
        in_specs=[pl.BlockSpec((bm, k), lambda i: (i, 0)), pl.BlockSpec((k, n), lambda i: (0, 0)),
                  pl.BlockSpec((1, n), lambda i: (0, 0))],
        out_specs=pl.BlockSpec((bm, n), lambda i: (i, 0)),
        out_shape=jax.ShapeDtypeStruct((m, n), BF16),
        compiler_params=_cparams("arbitrary"),
        name=name,
    )(x, w, g.reshape(1, n))


def _mm_q_up_kernel(x_ref, w_ref, cs_ref, sn_ref, q_ref):
    acc = jnp.dot(x_ref[...], w_ref[...], preferred_element_type=F32) * B_SCALE
    cs = cs_ref[...]
    sn = sn_ref[...]
    for h in range(HEAD_GROUP):
        lo = h * QK_HEAD_W
        q_ref[:, lo:lo + B_NOPE] = acc[:, lo:lo + B_NOPE].astype(BF16)
        q_ref[:, lo + B_NOPE:lo + QK_HEAD_W] = _rope_dup(acc[:, lo + B_NOPE:lo + QK_HEAD_W], cs, sn).astype(BF16)


def _mm_q_up(qlat, w_perm, cs, sn, *, bm, name):
    m, k = qlat.shape
    ng = B_HEADS // HEAD_GROUP
    wn = HEAD_GROUP * QK_HEAD_W
    nt = cs.shape[0] // bm
    return pl.pallas_call(
        _mm_q_up_kernel,
        grid=(m // bm, ng),
        in_specs=[pl.BlockSpec((bm, k), lambda i, j: (i, 0)),
                  pl.BlockSpec((k, wn), lambda i, j: (0, j)),
                  pl.BlockSpec((bm, LANES), lambda i, j: (i % nt, 0)),
                  pl.BlockSpec((bm, LANES), lambda i, j: (i % nt, 0))],
        out_specs=pl.BlockSpec((bm, wn), lambda i, j: (i, j)),
        out_shape=jax.ShapeDtypeStruct((m, B_HEADS * QK_HEAD_W), BF16),
        compiler_params=_cparams("arbitrary", "arbitrary"),
        name=name,
    )(qlat, w_perm, cs, sn)


def _mm_heads_kernel(x_ref, w_ref, o_ref):
    o_ref[...] = jnp.dot(x_ref[...].astype(BF16), w_ref[...], preferred_element_type=F32).astype(o_ref.dtype)


def _mm_q_absorb(qcat, w_ukt, *, name):
    m = qcat.shape[0]
    return pl.pallas_call(
        _mm_heads_kernel,
        grid=(B_HEADS,),
        in_specs=[pl.BlockSpec((m, B_NOPE), lambda h: (0, 2 * h)),
                  pl.BlockSpec((None, B_NOPE, KV_LORA), lambda h: (h, 0, 0))],
        out_specs=pl.BlockSpec((m, KV_LORA), lambda h: (0, h)),
        out_shape=jax.ShapeDtypeStruct((m, B_HEADS * KV_LORA), BF16),
        compiler_params=_cparams("arbitrary"),
        name=name,
    )(qcat, w_ukt)


def _mm_o_up_kernel(x_ref, w_ref, z_ref, o_ref):
    acc = jnp.dot(x_ref[...].astype(BF16), w_ref[...], preferred_element_type=F32)
    o_ref[...] = (acc * _silu(z_ref[...].astype(F32))).astype(o_ref.dtype)


def _mm_o_up(olat, w_uv, z, *, name):
    m = olat.shape[0]
    return pl.pallas_call(
        _mm_o_up_kernel,
        grid=(B_HEADS,),
        in_specs=[pl.BlockSpec((m, KV_LORA), lambda h: (0, h)),
                  pl.BlockSpec((None, KV_LORA, B_VDIM), lambda h: (h, 0, 0)),
                  pl.BlockSpec((m, B_VDIM), lambda h: (0, h))],
        out_specs=pl.BlockSpec((m, B_VDIM), lambda h: (0, h)),
        out_shape=jax.ShapeDtypeStruct((m, B_HEADS * B_VDIM), BF16),
        compiler_params=_cparams("arbitrary"),
        name=name,
    )(olat, w_uv, z)


def _mlstm_gate_terms(lf_row, gi_row, m_prev):
    length = lf_row.shape[1]
    t_idx = lax.broadcasted_iota(jnp.int32, (length, length), 0)
    s_idx = lax.broadcasted_iota(jnp.int32, (length, length), 1)
    causal = s_idx <= t_idx
    eye = s_idx == t_idx
    b_col = jnp.sum(jnp.where(causal, lf_row, 0.0), axis=1, keepdims=True)
    b_row = jnp.sum(jnp.where(eye, b_col, 0.0), axis=0, keepdims=True)
    gi_col = jnp.sum(jnp.where(eye, gi_row, 0.0), axis=1, keepdims=True)
    log_d = jnp.where(causal, b_col - b_row + gi_row, -jnp.inf)
    m_t = jnp.maximum(jnp.max(log_d, axis=1, keepdims=True), b_col + m_prev)
    w_inter = jnp.exp(b_col + m_prev - m_t)
    d = jnp.exp(log_d - m_t)
    b_end = b_row[:, length - 1:length]
    log_w_row = b_end - b_row + gi_row
    m_new = jnp.maximum(b_end + m_prev, jnp.max(log_w_row, axis=1, keepdims=True))
    w_c = jnp.exp(b_end + m_prev - m_new)
    w_s_col = jnp.exp(b_end - b_col + gi_col - m_new)
    return d, m_t, w_inter, w_s_col, w_c, m_new


def _mlstm_out_gate(h, norm_g, o, z):
    hn = h * lax.rsqrt(jnp.mean(h * h, axis=1, keepdims=True) + RMS_EPS) * norm_g
    return hn * (jax.nn.sigmoid(o) * _silu(z))


def _mlstm_prompt_kernel(q_ref, k_ref, v_ref, o_ref, z_ref, gm_ref, gc_ref, ng_ref,
                         hg_ref, c_out_ref, n_out_ref, m_out_ref, *, n_chunks):
    dqk = q_ref.shape[1]
    dv = v_ref.shape[1]
    scale = dqk ** -0.5
    norm_g = ng_ref[...]

    def run_chunk(rows, gi_row, lf_row, c_prev, n_prev, m_prev):
        q = q_ref[rows, :]
        k = k_ref[rows, :]
        v = v_ref[rows, :]
        d, m_t, w_inter, w_s_col, w_c, m_new = _mlstm_gate_terms(lf_row, gi_row, m_prev)
        s = lax.dot_general(q, k, NT_DIMS, preferred_element_type=F32) * (scale * d)
        q_c = jnp.dot(q, c_prev.astype(BF16), preferred_element_type=F32)
        s_v = jnp.dot(s.astype(BF16), v, preferred_element_type=F32)
        w_q = w_inter * scale
        num = w_q * q_c + s_v
        q_n = jnp.sum(q.astype(F32) * n_prev, axis=1, keepdims=True)
        den = w_q * q_n + jnp.sum(s, axis=1, keepdims=True)
        h = num / jnp.maximum(jnp.abs(den), jnp.exp(-m_t))
        w_v = (w_s_col * v.astype(F32)).astype(BF16)
        c_new = w_c * c_prev + lax.dot_general(k, w_v, TN_DIMS, preferred_element_type=F32)
        n_new = w_c * n_prev + jnp.sum(w_s_col * k.astype(F32), axis=0, keepdims=True)
        out = _mlstm_out_gate(h, norm_g, o_ref[rows, :].astype(F32), z_ref[rows, :].astype(F32))
        hg_ref[rows, :] = out.astype(hg_ref.dtype)
        return c_new, n_new, m_new

    state = (jnp.zeros((dqk, dv), F32), jnp.zeros((1, dqk), F32), jnp.zeros((1, 1), F32))
    state = run_chunk(pl.ds(0, N_META), gm_ref[0:1, :], gm_ref[1:2, :], *state)

    def body(c, carry):
        g = gc_ref[c]
        r0 = pl.multiple_of(N_META + c * A_CHUNK, BF16_SUBLANES)
        return run_chunk(pl.ds(r0, A_CHUNK), g[0:1, :], g[1:2, :], *carry)

    c_fin, n_fin, m_fin = lax.fori_loop(0, n_chunks, body, state)
    c_out_ref[...] = c_fin
    n_out_ref[...] = n_fin
    m_out_ref[...] = jnp.broadcast_to(m_fin, m_out_ref.shape)


def _mlstm_prompt(u, g_meta, g_chunks, norm_g, *, dqk, dv):
    bsz, t, _ = u.shape
    n_chunks = g_chunks.shape[2]
    qk_blocks = A_HEADS
    v_base = 2 * A_HEADS * dqk // dv
    tok = lambda width, base: pl.BlockSpec((None, t, width), lambda b, h: (b, 0, base + h))
    return pl.pallas_call(
        functools.partial(_mlstm_prompt_kernel, n_chunks=n_chunks),
        grid=(bsz, A_HEADS),
        in_specs=[tok(dqk, 0), tok(dqk, qk_blocks), tok(dv, v_base), tok(dv, v_base + A_HEADS),
                  tok(dv, v_base + 2 * A_HEADS),
                  pl.BlockSpec((None, None, 2, N_META), lambda b, h: (b, h, 0, 0)),
                  pl.BlockSpec((None, None, n_chunks, 2, A_CHUNK), lambda b, h: (b, h, 0, 0, 0)),
                  pl.BlockSpec((None, 1, dv), lambda b, h: (h, 0, 0))],
        out_specs=[pl.BlockSpec((None, t, dv), lambda b, h: (b, 0, h)),
                   pl.BlockSpec((None, None, dqk, dv), lambda b, h: (b, h, 0, 0)),
                   pl.BlockSpec((None, None, 1, dqk), lambda b, h: (b, h, 0, 0)),
                   pl.BlockSpec((None, None, 1, LANES), lambda b, h: (b, h, 0, 0))],
        out_shape=[jax.ShapeDtypeStruct((bsz, t, A_HEADS * dv), BF16),
                   jax.ShapeDtypeStruct((bsz, A_HEADS, dqk, dv), F32),
                   jax.ShapeDtypeStruct((bsz, A_HEADS, 1, dqk), F32),
                   jax.ShapeDtypeStruct((bsz, A_HEADS, 1, LANES), F32)],
        compiler_params=_cparams("arbitrary", "arbitrary"),
        name="mlstm_prompt",
    )(u, u, u, u, u, g_meta, g_chunks, norm_g.reshape(A_HEADS, 1, dv))


def _rows_from(parts, width):
    r_idx = lax.broadcasted_iota(jnp.int32, (len(parts), width), 0)
    out = jnp.zeros((len(parts), width), F32)
    for i, p in enumerate(parts):
        out = jnp.where(r_idx == i, p, out)
    return out


def _cols_from(parts, height):
    c_idx = lax.broadcasted_iota(jnp.int32, (height, len(parts)), 1)
    out = jnp.zeros((height, len(parts)), F32)
    for i, p in enumerate(parts):
        out = jnp.where(c_idx == i, p, out)
    return out


def _mlstm_sample_kernel(u_ref, qt_ref, kt_ref, g_ref, ng_ref, c_ref, n_ref, m_ref,
                         hg_ref, c_out_ref, n_out_ref, m_out_ref, *, dqk, dv):
    s_len = u_ref.shape[0]
    scale = dqk ** -0.5
    k_base = A_HEADS * dqk
    v_base = 2 * A_HEADS * dqk
    o_base = v_base + A_HEADS * dv
    z_base = o_base + A_HEADS * dv
    for h in range(A_HEADS):
        q = u_ref[:, h * dqk:(h + 1) * dqk]
        k = u_ref[:, k_base + h * dqk:k_base + (h + 1) * dqk]
        v = u_ref[:, v_base + h * dv:v_base + (h + 1) * dv]
        q_t = qt_ref[h]
        k_t = kt_ref[h]
        c_prev = c_ref[h]
        n_prev = n_ref[h:h + 1, :]
        m_prev = m_ref[h:h + 1, :]
        gi_row = g_ref[h:h + 1, :]
        lf_row = g_ref[A_HEADS + h:A_HEADS + h + 1, :]
        d, m_t, w_inter, w_s_col, w_c, m_new = _mlstm_gate_terms(lf_row, gi_row, m_prev)
        qk = _cols_from([jnp.sum(q * k[j:j + 1, :], axis=1, keepdims=True) for j in range(s_len)], s_len)
        s = qk * (scale * d)
        q_c = _rows_from([jnp.sum(q_t[:, j:j + 1] * c_prev, axis=0, keepdims=True) for j in range(s_len)], dv)
        s_v = jnp.zeros((s_len, dv), F32)
        for j in range(s_len):
            s_v = s_v + s[:, j:j + 1] * v[j:j + 1, :]
        w_q = w_inter * scale
        num = w_q * q_c + s_v
        q_n = jnp.sum(q * n_prev, axis=1, keepdims=True)
        den = w_q * q_n + jnp.sum(s, axis=1, keepdims=True)
        hh = num / jnp.maximum(jnp.abs(den), jnp.exp(-m_t))
        w_v = w_s_col * v
        c_new = w_c * c_prev
        for j in range(s_len):
            c_new = c_new + k_t[:, j:j + 1] * w_v[j:j + 1, :]
        n_new = w_c * n_prev + jnp.sum(w_s_col * k, axis=0, keepdims=True)
        out = _mlstm_out_gate(hh, ng_ref[h:h + 1, :], u_ref[:, o_base + h * dv:o_base + (h + 1) * dv],
                              u_ref[:, z_base + h * dv:z_base + (h + 1) * dv])
        hg_ref[:, h * dv:(h + 1) * dv] = out
        c_out_ref[h] = c_new
        n_out_ref[h:h + 1, :] = n_new
        m_out_ref[h:h + 1, :] = m_new


def _mlstm_sample(u, q_t, k_t, g_rows, norm_g, c_prev, n_prev, m_prev, *, dqk, dv):
    dbs, s_len, width = u.shape
    seq3 = lambda *shape: pl.BlockSpec((None,) + shape, lambda b: (b,) + (0,) * len(shape))
    return pl.pallas_call(
        functools.partial(_mlstm_sample_kernel, dqk=dqk, dv=dv),
        grid=(dbs,),
        in_specs=[seq3(s_len, width), seq3(A_HEADS, dqk, s_len), seq3(A_HEADS, dqk, s_len),
                  seq3(2 * A_HEADS, s_len), pl.BlockSpec((A_HEADS, dv), lambda b: (0, 0)),
                  seq3(A_HEADS, dqk, dv), seq3(A_HEADS, dqk), seq3(A_HEADS, 1)],
        out_specs=[seq3(s_len, A_HEADS * dv), seq3(A_HEADS, dqk, dv), seq3(A_HEADS, dqk), seq3(A_HEADS, 1)],
        out_shape=[jax.ShapeDtypeStruct((dbs, s_len, A_HEADS * dv), F32),
                   jax.ShapeDtypeStruct((dbs, A_HEADS, dqk, dv), F32),
                   jax.ShapeDtypeStruct((dbs, A_HEADS, dqk), F32),
                   jax.ShapeDtypeStruct((dbs, A_HEADS, 1), F32)],
        compiler_params=_cparams("arbitrary"),
        name="mlstm_sample",
    )(u, q_t, k_t, g_rows, norm_g, c_prev, n_prev, m_prev)


def _attn_prompt_kernel(q_ref, k_ref, v_ref, z_ref, o_ref, *, n_tiles):
    tq = q_ref.shape[0] // n_tiles
    row = lax.broadcasted_iota(jnp.int32, (tq, tq), 0)
    col = lax.broadcasted_iota(jnp.int32, (tq, tq), 1)
    for i in range(n_tiles):
        rows = pl.ds(i * tq, tq)
        q = q_ref[rows, :]
        m = jnp.full((tq, 1), -jnp.inf, F32)
        l = jnp.zeros((tq, 1), F32)
        acc = jnp.zeros((tq, v_ref.shape[1]), F32)
        for j in range(i + 1):
            keys = pl.ds(j * tq, tq)
            s = lax.dot_general(q, k_ref[keys, :], NT_DIMS, preferred_element_type=F32)
            if j == i:
                s = jnp.where(col <= row, s, -jnp.inf)
            m_new = jnp.maximum(m, jnp.max(s, axis=1, keepdims=True))
            alpha = jnp.exp(m - m_new)
            p = jnp.exp(s - m_new)
            l = alpha * l + jnp.sum(p, axis=1, keepdims=True)
            acc = alpha * acc + jnp.dot(p.astype(BF16), v_ref[keys, :], preferred_element_type=F32)
            m = m_new
        o_ref[rows, :] = ((acc / l) * _silu(z_ref[rows, :].astype(F32))).astype(o_ref.dtype)


def _attn_prompt(qcat, kcat, v, z, *, n_tiles):
    bsz, t, _ = qcat.shape
    qk = pl.BlockSpec((None, t, QK_HEAD_W), lambda b, h: (b, 0, h))
    hv = pl.BlockSpec((None, t, B_VDIM), lambda b, h: (b, 0, h))
    return pl.pallas_call(
        functools.partial(_attn_prompt_kernel, n_tiles=n_tiles),
        grid=(bsz, B_HEADS),
        in_specs=[qk, qk, hv, hv],
        out_specs=hv,
        out_shape=jax.ShapeDtypeStruct((bsz, t, B_HEADS * B_VDIM), BF16),
        compiler_params=_cparams("arbitrary", "arbitrary"),
        name="mla_attn_prompt",
    )(qcat, kcat, v, z)


def _attn_sample_kernel(pt_ref, ql_ref, qpe_ref, cn_ref, pn_ref, *rest, n_pages):
    del pt_ref
    ckv_refs = rest[:n_pages]
    kpe_refs = rest[n_pages:2 * n_pages]
    o_ref = rest[2 * n_pages]
    cbuf, pbuf, m_sc, l_sc, acc_sc = rest[2 * n_pages + 1:]
    j = pl.program_id(1)

    @pl.when(j == 0)
    def _():
        m_sc[...] = jnp.full(m_sc.shape, -jnp.inf, F32)
        l_sc[...] = jnp.zeros(l_sc.shape, F32)
        acc_sc[...] = jnp.zeros(acc_sc.shape, F32)
        pbuf[...] = jnp.zeros(pbuf.shape, BF16)

    for g in range(n_pages):
        rows = pl.ds(g * PAGE_SIZE, PAGE_SIZE)
        cbuf[rows, :] = ckv_refs[g][...].astype(BF16)
        pbuf[rows, 0:B_ROPE] = kpe_refs[g][...].astype(BF16)

    ql = ql_ref[...]
    qpe = qpe_ref[...]
    c_all = cbuf[...]
    s = (lax.dot_general(ql, c_all, NT_DIMS, preferred_element_type=F32)
         + lax.dot_general(qpe, pbuf[...], NT_DIMS, preferred_element_type=F32))
    m_prev = m_sc[...]
    m_new = jnp.maximum(m_prev, jnp.max(s, axis=1, keepdims=True))
    alpha = jnp.exp(m_prev - m_new)
    p = jnp.exp(s - m_new)
    l_sc[...] = alpha * l_sc[...] + jnp.sum(p, axis=1, keepdims=True)
    acc_sc[...] = alpha * acc_sc[...] + jnp.dot(p.astype(BF16), c_all, preferred_element_type=F32)
    m_sc[...] = m_new

    @pl.when(j == pl.num_programs(1) - 1)
    def _():
        n_rows = ql.shape[0]
        s_len = cn_ref.shape[0]
        tok = lax.broadcasted_iota(jnp.int32, (n_rows, 1), 0) // B_HEADS
        qlf = ql.astype(F32)
        qpf = qpe.astype(F32)[:, :B_ROPE]
        m_run = m_sc[...]
        l_run = l_sc[...]
        acc = acc_sc[...]
        for t in range(s_len):
            c_row = cn_ref[t:t + 1, :]
            sc = (jnp.sum(qlf * c_row, axis=1, keepdims=True)
                  + jnp.sum(qpf * pn_ref[t:t + 1, :], axis=1, keepdims=True))
            sc = jnp.where(tok >= t, sc, -jnp.inf)
            m_nxt = jnp.maximum(m_run, sc)
            a = jnp.exp(m_run - m_nxt)
            pt = jnp.exp(sc - m_nxt)
            l_run = a * l_run + pt
            acc = a * acc + pt * c_row
            m_run = m_nxt
        o_ref[...] = (acc / l_run).astype(o_ref.dtype)


def _attn_sample(page_table, ql, qpe, c_new, pe_new, cache_ckv, cache_kpe, *, n_pages):
    dbs, n_rows, _ = ql.shape
    s_len = c_new.shape[1]
    n_steps = page_table.shape[1] // n_pages
    seq = lambda *shape: pl.BlockSpec((None,) + shape, lambda b, j, pt: (b,) + (0,) * len(shape))

    def page(width, g):
        return pl.BlockSpec((None, PAGE_SIZE, width), lambda b, j, pt: (pt[b, j * n_pages + g], 0, 0))

    grid_spec = pltpu.PrefetchScalarGridSpec(
        num_scalar_prefetch=1,
        grid=(dbs, n_steps),
        in_specs=([seq(n_rows, KV_LORA), seq(n_rows, LANES), seq(s_len, KV_LORA), seq(s_len, B_ROPE)]
                  + [page(KV_LORA, g) for g in range(n_pages)]
                  + [page(B_ROPE, g) for g in range(n_pages)]),
        out_specs=pl.BlockSpec((None, n_rows, KV_LORA), lambda b, j, pt: (b, 0, 0)),
        scratch_shapes=[pltpu.VMEM((n_pages * PAGE_SIZE, KV_LORA), BF16),
                        pltpu.VMEM((n_pages * PAGE_SIZE, LANES), BF16),
                        pltpu.VMEM((n_rows, 1), F32), pltpu.VMEM((n_rows, 1), F32),
                        pltpu.VMEM((n_rows, KV_LORA), F32)],
    )
    return pl.pallas_call(
        functools.partial(_attn_sample_kernel, n_pages=n_pages),
        grid_spec=grid_spec,
        out_shape=jax.ShapeDtypeStruct((dbs, n_rows, KV_LORA), BF16),
        compiler_params=_cparams("arbitrary", "arbitrary"),
        name="mla_attn_sample",
    )(page_table, ql, qpe, c_new, pe_new, *([cache_ckv] * n_pages), *([cache_kpe] * n_pages))


def _rope_tables(pos):
    half = B_ROPE // 2
    inv = ROPE_BASE ** (-jnp.arange(half, dtype=F32) / half)
    ang = pos[:, None] * inv[None, :]
    cos, sin = jnp.cos(ang), jnp.sin(ang)
    zero = jnp.zeros_like(cos)
    return (jnp.concatenate([cos, cos, zero, zero], axis=1),
            jnp.concatenate([-sin, sin, zero, zero], axis=1))


def _dup_rope_cols(w):
    half = B_ROPE // 2
    x1, x2 = w[..., :half], w[..., half:]
    return jnp.concatenate([x1, x2, x2, x1], axis=-1)


def kernel(x_prompt, x_sample, state_mlstm_C, state_mlstm_n, state_mlstm_m, cache_ckv, cache_kpe, page_table,
           meta_tokens, a_w_in, a_b_gates, a_norm_g, a_w_out, a_ln_g, a_ln_b, kv_w_down, kv_norm_g, kv_w_up,
           b_w_in, b_q_norm_g, b_w_qb, b_w_out, b_ln_g, b_ln_b):
    bsz, seq, d_model = x_prompt.shape
    dbs, s_len, _ = x_sample.shape
    n_a, n_b = a_w_in.shape[0], b_w_in.shape[0]
    alpha = (2 * (n_a + n_b)) ** 0.25
    dqk = d_model // (2 * A_HEADS)
    dv = d_model // A_HEADS
    qk_w, v_w = A_HEADS * dqk, A_HEADS * dv
    main_w = 2 * qk_w + 3 * v_w
    t = seq + N_META
    n_chunks = seq // A_CHUNK
    past = page_table.shape[1] * PAGE_SIZE
    mp, ms = bsz * t, dbs * s_len

    n_tiles = 3 if t % (3 * BF16_SUBLANES) == 0 else 1
    bm_p = t // n_tiles
    bm_ln = _pick_block(mp, 384)
    bm_big = _pick_block(mp, 1376)
    bm_s = _pick_block(ms, 512)
    n_pages = min(16, page_table.shape[1])

    w_up3 = kv_w_up.reshape(KV_LORA, B_HEADS, B_NOPE + B_VDIM)
    w_uk, w_uv = w_up3[..., :B_NOPE], w_up3[..., B_NOPE:]
    ng = B_HEADS // HEAD_GROUP
    w_kv_perm = jnp.concatenate(
        [w_uk.reshape(KV_LORA, ng, HEAD_GROUP * B_NOPE), w_uv.reshape(KV_LORA, ng, HEAD_GROUP * B_VDIM)],
        axis=2).reshape(KV_LORA, -1).astype(BF16)
    w_ukt = jnp.transpose(w_uk, (1, 2, 0)).astype(BF16)
    w_uvh = jnp.transpose(w_uv, (1, 0, 2)).astype(BF16)
    w_dn_ext = jnp.concatenate([kv_w_down[:, :KV_LORA], _dup_rope_cols(kv_w_down[:, KV_LORA:])], axis=1).astype(BF16)

    pos_p = jnp.arange(t, dtype=F32)
    pos_s = past + jnp.arange(s_len, dtype=F32)
    cs_p, sn_p = _rope_tables(pos_p)
    cs_s, sn_s = (jnp.tile(a, (bm_s // s_len, 1)) for a in _rope_tables(pos_s))

    meta = jnp.broadcast_to(meta_tokens.astype(x_prompt.dtype)[None], (bsz, N_META, d_model))
    xp = jnp.concatenate([meta, x_prompt], axis=1).reshape(mp, d_model)
    xs = x_sample.reshape(ms, d_model)
    xp16 = xp.astype(BF16)
    xs16 = xs.astype(BF16)

    st_p, st_s = [], []
    for i in range(n_a):
        w_main = a_w_in[i][:, :main_w].astype(BF16)
        w_g = jnp.pad(a_w_in[i][:, main_w:], ((0, 0), (0, LANES - 2 * A_HEADS))).astype(BF16)
        b_g = jnp.pad(a_b_gates[i], (0, LANES - 2 * A_HEADS)).reshape(1, LANES)
        w_out = a_w_out[i].astype(BF16)

        u = _mm_plain(xp16, w_main, BF16, bm=bm_big, bn=1024, name="a_in_prompt")
        gates = _mm_gates(xp16, w_g, b_g, bm=bm_big, name="a_gates_prompt")
        g4 = gates[:, :2 * A_HEADS].reshape(bsz, t, 2, A_HEADS).transpose(0, 3, 2, 1)
        g_meta = g4[..., :N_META]
        g_chunks = g4[..., N_META:].reshape(bsz, A_HEADS, 2, n_chunks, A_CHUNK).transpose(0, 1, 3, 2, 4)
        hg, c_p, n_p, m_p = _mlstm_prompt(u.reshape(bsz, t, main_w), g_meta, g_chunks, a_norm_g[i], dqk=dqk, dv=dv)
        xp, xp16 = _mm_resid_ln(hg.reshape(mp, v_w), w_out, xp, a_ln_g[i], a_ln_b[i], alpha, (F32, BF16),
                                bm=bm_ln, name="a_out_prompt")
        st_p.append((c_p, n_p.reshape(bsz, A_HEADS, dqk), m_p[:, :, 0, 0]))

        us = _mm_plain(xs16, w_main, F32, bm=bm_s, bn=1024, name="a_in_sample").reshape(dbs, s_len, main_w)
        gs = _mm_gates(xs16, w_g, b_g, bm=bm_s, name="a_gates_sample")
        gs_rows = gs[:, :2 * A_HEADS].reshape(dbs, s_len, 2 * A_HEADS).transpose(0, 2, 1)
        q_t = us[..., :qk_w].reshape(dbs, s_len, A_HEADS, dqk).transpose(0, 2, 3, 1)
        k_t = us[..., qk_w:2 * qk_w].reshape(dbs, s_len, A_HEADS, dqk).transpose(0, 2, 3, 1)
        hgs, c_s, n_s, m_s = _mlstm_sample(us, q_t, k_t, gs_rows, a_norm_g[i], state_mlstm_C[:, i],
                                           state_mlstm_n[:, i], state_mlstm_m[:, i][..., None], dqk=dqk, dv=dv)
        xs, xs16 = _mm_resid_ln(hgs.reshape(ms, v_w), w_out, xs, a_ln_g[i], a_ln_b[i], alpha, (F32, BF16),
                                bm=bm_s, name="a_out_sample")
        st_s.append((c_s, n_s, m_s[..., 0]))

    for j in range(n_b):
        if j == 0:
            ckv_p, kpe_p, ckv16_p, kpe16_p = _mm_kv_down(xp16, w_dn_ext, kv_norm_g, cs_p, sn_p, bm=bm_p,
                                                         name="kv_down_prompt")
            ckv_s, kpe_s, _, _ = _mm_kv_down(xs16, w_dn_ext, kv_norm_g, cs_s, sn_s, bm=bm_s, name="kv_down_sample")
            kcat, vcat = _mm_kv_up(ckv16_p, w_kv_perm, kpe16_p, bm=bm_p, name="kv_up_prompt")
        w_ql = b_w_in[j][:, :Q_LORA].astype(BF16)
        w_z = b_w_in[j][:, Q_LORA:].astype(BF16)
        w_qb3 = b_w_qb[j].reshape(Q_LORA, B_HEADS, B_NOPE + B_ROPE)
        w_q_perm = jnp.concatenate([w_qb3[..., :B_NOPE], _dup_rope_cols(w_qb3[..., B_NOPE:])],
                                   axis=2).reshape(Q_LORA, -1).astype(BF16)
        w_out = b_w_out[j].astype(BF16)

        qlat = _mm_rms(xp16, w_ql, b_q_norm_g[j], bm=bm_big, name="b_qlat_prompt")
        z = _mm_plain(xp16, w_z, BF16, bm=bm_big, bn=1024, name="b_z_prompt")
        qcat = _mm_q_up(qlat, w_q_perm, cs_p, sn_p, bm=bm_p, name="b_q_up_prompt")
        og = _attn_prompt(qcat.reshape(bsz, t, -1), kcat.reshape(bsz, t, -1), vcat.reshape(bsz, t, -1),
                          z.reshape(bsz, t, -1), n_tiles=n_tiles)
        xp, xp16 = _mm_resid_ln(og.reshape(mp, -1), w_out, xp, b_ln_g[j], b_ln_b[j], alpha, (F32, BF16),
                                bm=bm_ln, name="b_out_prompt")

        qlat_s = _mm_rms(xs16, w_ql, b_q_norm_g[j], bm=bm_s, name="b_qlat_sample")
        z_s = _mm_plain(xs16, w_z, BF16, bm=bm_s, bn=1024, name="b_z_sample")
        qcat_s = _mm_q_up(qlat_s, w_q_perm, cs_s, sn_s, bm=bm_s, name="b_q_up_sample")
        ql = _mm_q_absorb(qcat_s, w_ukt, name="b_q_absorb_sample")
        qpe = qcat_s.reshape(ms, B_HEADS, QK_HEAD_W)[:, :, B_NOPE:]
        olat = _attn_sample(page_table, ql.reshape(dbs, s_len * B_HEADS, KV_LORA),
                            qpe.reshape(dbs, s_len * B_HEADS, LANES),
                            ckv_s.reshape(dbs, s_len, KV_LORA), kpe_s.reshape(dbs, s_len, B_ROPE),
                            cache_ckv, cache_kpe, n_pages=n_pages)
        ogs = _mm_o_up(olat.reshape(ms, B_HEADS * KV_LORA), w_uvh, z_s, name="b_o_up_sample")
        xs, xs16 = _mm_resid_ln(ogs, w_out, xs, b_ln_g[j], b_ln_b[j], alpha, (F32, BF16), bm=bm_s,
                                name="b_out_sample")

    stack = lambda parts, k: jnp.stack([p[k] for p in parts], axis=1)
    return (xp.reshape(bsz, t, d_model)[:, N_META:], xs.reshape(dbs, s_len, d_model),
            stack(st_p, 0), stack(st_p, 1), stack(st_p, 2),
            ckv_p.reshape(bsz, t, KV_LORA), kpe_p.reshape(bsz, t, B_ROPE),
            stack(st_s, 0), stack(st_s, 1), stack(st_s, 2),
            ckv_s.reshape(dbs, s_len, KV_LORA), kpe_s.reshape(dbs, s_len, B_ROPE))
```

```python
import functools

import jax
import jax.numpy as jnp
from jax import lax
from jax.experimental import pallas as pl
from jax.experimental.pallas import tpu as pltpu

F32 = jnp.float32
BF16 = jnp.bfloat16

N_META = 16
A_HEADS = 8
A_CHUNK = 64
IGATE_CAP = 15.0
B_HEADS = 16
B_NOPE = 128
B_ROPE = 64
B_VDIM = 128
KV_LORA = 512
Q_LORA = 512
PAGE_SIZE = 128
ROPE_BASE = 10000.0
LN_EPS = 1e-5
RMS_EPS = 1e-6
B_SCALE = (B_NOPE + B_ROPE) ** -0.5

LANES = 128
BF16_SUBLANES = 16
VMEM_LIMIT_BYTES = 56 * 1024 * 1024
HEAD_GROUP = 4
MLSTM_HEAD_GROUP = 4
SAMPLE_CHUNK = BF16_SUBLANES
QK_HEAD_W = 2 * LANES

NT_DIMS = (((1,), (1,)), ((), ()))
TN_DIMS = (((0,), (0,)), ((), ()))


def _cparams(*sem):
    return pltpu.CompilerParams(dimension_semantics=sem, vmem_limit_bytes=VMEM_LIMIT_BYTES)


def _pick_block(m, target, mult=BF16_SUBLANES):
    best = None
    for d in range(mult, min(m, target) + 1, mult):
        if m % d == 0:
            best = d
    return best if best is not None else m


def _layer_norm(y, g, b):
    mu = jnp.mean(y, axis=-1, keepdims=True)
    yc = y - mu
    var = jnp.mean(yc * yc, axis=-1, keepdims=True)
    return yc * lax.rsqrt(var + LN_EPS) * g + b


def _rms_norm(y, g):
    return y * lax.rsqrt(jnp.mean(y * y, axis=-1, keepdims=True) + RMS_EPS) * g


def _rope_dup(v, cs, sn):
    return v * cs + pltpu.roll(v, 2 * (B_ROPE // 2), axis=1) * sn


def _silu(z):
    return z * jax.nn.sigmoid(z)


def _mm_plain_kernel(x_ref, w_ref, o_ref):
    acc = jnp.dot(x_ref[...].astype(BF16), w_ref[...], preferred_element_type=F32)
    o_ref[...] = acc.astype(o_ref.dtype)


def _mm_plain(x, w, out_dtype, *, bm, bn, name):
    m, k = x.shape
    n = w.shape[1]
    return pl.pallas_call(
        _mm_plain_kernel,
        grid=(n // bn, m // bm),
        in_specs=[pl.BlockSpec((bm, k), lambda j, i: (i, 0)),
                  pl.BlockSpec((k, bn), lambda j, i: (0, j))],
        out_specs=pl.BlockSpec((bm, bn), lambda j, i: (i, j)),
        out_shape=jax.ShapeDtypeStruct((m, n), out_dtype),
        compiler_params=_cparams("arbitrary", "arbitrary"),
        name=name,
    )(x, w)


def _mm_gates_kernel(x_ref, w_ref, b_ref, o_ref):
    a = jnp.dot(x_ref[...].astype(BF16), w_ref[...], preferred_element_type=F32) + b_ref[...]
    lane = lax.broadcasted_iota(jnp.int32, a.shape, 1)
    gi = IGATE_CAP * jnp.tanh(a / IGATE_CAP)
    lf = jnp.minimum(a, 0.0) - jnp.log1p(jnp.exp(-jnp.abs(a)))
    o_ref[...] = jnp.where(lane < A_HEADS, gi, lf)


def _mm_gates(x, w, b, *, bm, name):
    m, k = x.shape
    return pl.pallas_call(
        _mm_gates_kernel,
        grid=(m // bm,),
        in_specs=[pl.BlockSpec((bm, k), lambda i: (i, 0)),
                  pl.BlockSpec((k, LANES), lambda i: (0, 0)),
                  pl.BlockSpec((1, LANES), lambda i: (0, 0))],
        out_specs=pl.BlockSpec((bm, LANES), lambda i: (i, 0)),
        out_shape=jax.ShapeDtypeStruct((m, LANES), F32),
        compiler_params=_cparams("arbitrary"),
        name=name,
    )(x, w, b)


def _mm_resid_ln_kernel(x_ref, w_ref, r_ref, g_ref, b_ref, *o_refs, alpha):
    acc = jnp.dot(x_ref[...].astype(BF16), w_ref[...], preferred_element_type=F32)
    y = _layer_norm(alpha * r_ref[...] + acc, g_ref[...], b_ref[...])
    for o_ref in o_refs:
        o_ref[...] = y.astype(o_ref.dtype)


def _mm_resid_ln(x, w, resid, g, b, alpha, out_dtypes, *, bm, name):
    m, k = x.shape
    n = w.shape[1]
    row = lambda i: (i, 0)
    fixed = lambda i: (0, 0)
    outs = pl.pallas_call(
        functools.partial(_mm_resid_ln_kernel, alpha=alpha),
        grid=(m // bm,),
        in_specs=[pl.BlockSpec((bm, k), row), pl.BlockSpec((k, n), fixed),
                  pl.BlockSpec((bm, n), row), pl.BlockSpec((1, n), fixed), pl.BlockSpec((1, n), fixed)],
        out_specs=[pl.BlockSpec((bm, n), row) for _ in out_dtypes],
        out_shape=[jax.ShapeDtypeStruct((m, n), dt) for dt in out_dtypes],
        compiler_params=_cparams("arbitrary"),
        name=name,
    )(x, w, resid, g.reshape(1, n), b.reshape(1, n))
    return outs


def _mm_kv_down_kernel(x_ref, w_ref, g_ref, cs_ref, sn_ref, ckv_ref, kpe_ref, ckv16_ref, kpe16_ref):
    acc = jnp.dot(x_ref[...].astype(BF16), w_ref[...], preferred_element_type=F32)
    ckv = _rms_norm(acc[:, :KV_LORA], g_ref[...])
    pe = _rope_dup(acc[:, KV_LORA:], cs_ref[...], sn_ref[...])
    ckv_ref[...] = ckv
    ckv16_ref[...] = ckv.astype(BF16)
    kpe_ref[...] = pe[:, :B_ROPE]
    kpe16_ref[...] = pe.astype(BF16)


def _mm_kv_down(x, w_ext, g, cs, sn, *, bm, name):
    m, k = x.shape
    n = w_ext.shape[1]
    nt = cs.shape[0] // bm
    row = lambda i: (i, 0)
    fixed = lambda i: (0, 0)
    tab = lambda i: (i % nt, 0)
    return pl.pallas_call(
        _mm_kv_down_kernel,
        grid=(m // bm,),
        in_specs=[pl.BlockSpec((bm, k), row), pl.BlockSpec((k, n), fixed), pl.BlockSpec((1, KV_LORA), fixed),
                  pl.BlockSpec((bm, LANES), tab), pl.BlockSpec((bm, LANES), tab)],
        out_specs=[pl.BlockSpec((bm, KV_LORA), row), pl.BlockSpec((bm, B_ROPE), row),
                   pl.BlockSpec((bm, KV_LORA), row), pl.BlockSpec((bm, LANES), row)],
        out_shape=[jax.ShapeDtypeStruct((m, KV_LORA), F32), jax.ShapeDtypeStruct((m, B_ROPE), F32),
                   jax.ShapeDtypeStruct((m, KV_LORA), BF16), jax.ShapeDtypeStruct((m, LANES), BF16)],
        compiler_params=_cparams("arbitrary"),
        name=name,
    )(x, w_ext, g.reshape(1, KV_LORA), cs, sn)


def _mm_kv_up_kernel(c_ref, w_ref, kpe_ref, k_ref, v_ref):
    acc = jnp.dot(c_ref[...], w_ref[...], preferred_element_type=F32)
    kpe = kpe_ref[...]
    for h in range(HEAD_GROUP):
        k_ref[:, h * QK_HEAD_W:h * QK_HEAD_W + B_NOPE] = acc[:, h * B_NOPE:(h + 1) * B_NOPE].astype(BF16)
        k_ref[:, h * QK_HEAD_W + B_NOPE:(h + 1) * QK_HEAD_W] = kpe
    v_ref[...] = acc[:, HEAD_GROUP * B_NOPE:].astype(BF16)


def _mm_kv_up(ckv16, w_perm, kpe16, *, bm, name):
    m, k = ckv16.shape
    ng = B_HEADS // HEAD_GROUP
    wn = HEAD_GROUP * (B_NOPE + B_VDIM)
    return pl.pallas_call(
        _mm_kv_up_kernel,
        grid=(m // bm, ng),
        in_specs=[pl.BlockSpec((bm, k), lambda i, j: (i, 0)),
                  pl.BlockSpec((k, wn), lambda i, j: (0, j)),
                  pl.BlockSpec((bm, LANES), lambda i, j: (i, 0))],
        out_specs=[pl.BlockSpec((bm, HEAD_GROUP * QK_HEAD_W), lambda i, j: (i, j)),
                   pl.BlockSpec((bm, HEAD_GROUP * B_VDIM), lambda i, j: (i, j))],
        out_shape=[jax.ShapeDtypeStruct((m, B_HEADS * QK_HEAD_W), BF16),
                   jax.ShapeDtypeStruct((m, B_HEADS * B_VDIM), BF16)],
        compiler_params=_cparams("arbitrary", "arbitrary"),
        name=name,
    )(ckv16, w_perm, kpe16)


def _mm_rms_kernel(x_ref, w_ref, g_ref, o_ref):
    acc = jnp.dot(x_ref[...].astype(BF16), w_ref[...], preferred_element_type=F32)
    o_ref[...] = _rms_norm(acc, g_ref[...]).astype(o_ref.dtype)


def _mm_rms(x, w, g, *, bm, name):
    m, k = x.shape
    n = w.shape[1]
    return pl.pallas_call(
        _mm_rms_kernel,
        grid=(m // bm,),
        in_specs=[pl.BlockSpec((bm, k), lambda i: (i, 0)), pl.BlockSpec((k, n), lambda i: (0, 0)),
                  pl.BlockSpec((1, n), lambda i: (0, 0))],
        out_specs=pl.BlockSpec((bm, n), lambda i: (i, 0)),
        out_shape=jax.ShapeDtypeStruct((m, n), BF16),
        compiler_params=_cparams("arbitrary"),
        name=name,
    )(x, w, g.reshape(1, n))


def _mm_q_up_kernel(x_ref, w_ref, cs_ref, sn_ref, q_ref):
    acc = jnp.dot(x_ref[...], w_ref[...], preferred_element_type=F32) * B_SCALE
    cs = cs_ref[...]
    sn = sn_ref[...]
    for h in range(HEAD_GROUP):
        lo = h * QK_HEAD_W
        q_ref[:, lo:lo + B_NOPE] = acc[:, lo:lo + B_NOPE].astype(BF16)
        q_ref[:, lo + B_NOPE:lo + QK_HEAD_W] = _rope_dup(acc[:, lo + B_NOPE:lo + QK_HEAD_W], cs, sn).astype(BF16)


def _mm_q_up(qlat, w_perm, cs, sn, *, bm, name):
    m, k = qlat.shape
    ng = B_HEADS // HEAD_GROUP
    wn = HEAD_GROUP * QK_HEAD_W
    nt = cs.shape[0] // bm
    return pl.pallas_call(
        _mm_q_up_kernel,
        grid=(m // bm, ng),
        in_specs=[pl.BlockSpec((bm, k), lambda i, j: (i, 0)),
                  pl.BlockSpec((k, wn), lambda i, j: (0, j)),
                  pl.BlockSpec((bm, LANES), lambda i, j: (i % nt, 0)),
                  pl.BlockSpec((bm, LANES), lambda i, j: (i % nt, 0))],
        out_specs=pl.BlockSpec((bm, wn), lambda i, j: (i, j)),
        out_shape=jax.ShapeDtypeStruct((m, B_HEADS * QK_HEAD_W), BF16),
        compiler_params=_cparams("arbitrary", "arbitrary"),
        name=name,
    )(qlat, w_perm, cs, sn)


def _mm_heads_kernel(x_ref, w_ref, o_ref):
    o_ref[...] = jnp.dot(x_ref[...].astype(BF16), w_ref[...], preferred_element_type=F32).astype(o_ref.dtype)


def _mm_q_absorb(qcat, w_ukt, *, name):
    m = qcat.shape[0]
    return pl.pallas_call(
        _mm_heads_kernel,
        grid=(B_HEADS,),
        in_specs=[pl.BlockSpec((m, B_NOPE), lambda h: (0, 2 * h)),
                  pl.BlockSpec((None, B_NOPE, KV_LORA), lambda h: (h, 0, 0))],
        out_specs=pl.BlockSpec((m, KV_LORA), lambda h: (0, h)),
        out_shape=jax.ShapeDtypeStruct((m, B_HEADS * KV_LORA), BF16),
        compiler_params=_cparams("arbitrary"),
        name=name,
    )(qcat, w_ukt)


def _mm_o_up_kernel(x_ref, w_ref, z_ref, o_ref):
    acc = jnp.dot(x_ref[...].astype(BF16), w_ref[...], preferred_element_type=F32)
    o_ref[...] = (acc * _silu(z_ref[...].astype(F32))).astype(o_ref.dtype)


def _mm_o_up(olat, w_uv, z, *, name):
    m = olat.shape[0]
    return pl.pallas_call(
        _mm_o_up_kernel,
        grid=(B_HEADS,),
        in_specs=[pl.BlockSpec((m, KV_LORA), lambda h: (0, h)),
                  pl.BlockSpec((None, KV_LORA, B_VDIM), lambda h: (h, 0, 0)),
                  pl.BlockSpec((m, B_VDIM), lambda h: (0, h))],
        out_specs=pl.BlockSpec((m, B_VDIM), lambda h: (0, h)),
        out_shape=jax.ShapeDtypeStruct((m, B_HEADS * B_VDIM), BF16),
        compiler_params=_cparams("arbitrary"),
        name=name,
    )(olat, w_uv, z)


def _mlstm_gate_terms(lf_row, gi_row, m_prev):
    length = lf_row.shape[1]
    t_idx = lax.broadcasted_iota(jnp.int32, (length, length), 0)
    s_idx = lax.broadcasted_iota(jnp.int32, (length, length), 1)
    causal = s_idx <= t_idx
    eye = s_idx == t_idx
    b_col = jnp.sum(jnp.where(causal, lf_row, 0.0), axis=1, keepdims=True)
    b_row = jnp.sum(jnp.where(eye, b_col, 0.0), axis=0, keepdims=True)
    gi_col = jnp.sum(jnp.where(eye, gi_row, 0.0), axis=1, keepdims=True)
    log_d = jnp.where(causal, b_col - b_row + gi_row, -jnp.inf)
    m_t = jnp.maximum(jnp.max(log_d, axis=1, keepdims=True), b_col + m_prev)
    w_inter = jnp.exp(b_col + m_prev - m_t)
    d = jnp.exp(log_d - m_t)
    b_end = b_row[:, length - 1:length]
    log_w_row = b_end - b_row + gi_row
    m_new = jnp.maximum(b_end + m_prev, jnp.max(log_w_row, axis=1, keepdims=True))
    w_c = jnp.exp(b_end + m_prev - m_new)
    w_s_col = jnp.exp(b_end - b_col + gi_col - m_new)
    return d, m_t, w_inter, w_s_col, w_c, m_new


def _mlstm_chunk_heads(heads, scale):
    stage_a = []
    for q, k, v, o, z, gi_row, lf_row, c_prev, n_prev, m_prev, norm_g in heads:
        terms = _mlstm_gate_terms(lf_row, gi_row, m_prev)
        qk = lax.dot_general(q, k, NT_DIMS, preferred_element_type=F32)
        q_c = jnp.dot(q, c_prev.astype(BF16), preferred_element_type=F32)
        stage_a.append((terms, qk, q_c))
    stage_b = []
    for (q, k, v, o, z, _, _, c_prev, n_prev, _, _), (terms, qk, q_c) in zip(heads, stage_a):
        d, m_t, w_inter, w_s_col, w_c, m_new = terms
        s = qk * (scale * d)
        s_v = jnp.dot(s.astype(BF16), v, preferred_element_type=F32)
        w_v = (w_s_col * v.astype(F32)).astype(BF16)
        k_v = lax.dot_general(k, w_v, TN_DIMS, preferred_element_type=F32)
        stage_b.append((s, s_v, k_v))
    results = []
    for (q, k, v, o, z, _, _, c_prev, n_prev, _, norm_g), (terms, _, q_c), (s, s_v, k_v) in zip(
            heads, stage_a, stage_b):
        d, m_t, w_inter, w_s_col, w_c, m_new = terms
        w_q = w_inter * scale
        num = w_q * q_c + s_v
        q_n = jnp.sum(q.astype(F32) * n_prev, axis=1, keepdims=True)
        den = w_q * q_n + jnp.sum(s, axis=1, keepdims=True)
        h = num / jnp.maximum(jnp.abs(den), jnp.exp(-m_t))
        c_new = w_c * c_prev + k_v
        n_new = w_c * n_prev + jnp.sum(w_s_col * k.astype(F32), axis=0, keepdims=True)
        hn = h * lax.rsqrt(jnp.mean(h * h, axis=1, keepdims=True) + RMS_EPS) * norm_g
        out = hn * (jax.nn.sigmoid(o.astype(F32)) * _silu(z.astype(F32)))
        results.append((out, c_new, n_new, m_new))
    return results


def _mlstm_prompt_kernel(q_ref, k_ref, v_ref, o_ref, z_ref, gm_ref, gc_ref, ng_ref,
                         hg_ref, c_ref, n_ref, m_out_ref, *, n_chunks, heads):
    dqk = q_ref.shape[1] // heads
    dv = v_ref.shape[1] // heads
    scale = dqk ** -0.5

    def run_chunk(rows, gates, m_prev):
        inputs = []
        for h in range(heads):
            qk_cols = slice(h * dqk, (h + 1) * dqk)
            v_cols = slice(h * dv, (h + 1) * dv)
            inputs.append((q_ref[rows, qk_cols], k_ref[rows, qk_cols], v_ref[rows, v_cols], o_ref[rows, v_cols],
                           z_ref[rows, v_cols], *gates(h), c_ref[h], n_ref[h], m_prev[h], ng_ref[h]))
        m_next = []
        for h, (out, c_new, n_new, m_new) in enumerate(_mlstm_chunk_heads(inputs, scale)):
            hg_ref[rows, h * dv:(h + 1) * dv] = out.astype(hg_ref.dtype)
            c_ref[h] = c_new
            n_ref[h] = n_new
            m_next.append(m_new)
        return tuple(m_next)

    c_ref[...] = jnp.zeros(c_ref.shape, F32)
    n_ref[...] = jnp.zeros(n_ref.shape, F32)
    m_state = tuple(jnp.zeros((1, 1), F32) for _ in range(heads))
    m_state = run_chunk(pl.ds(0, N_META), lambda h: (gm_ref[h, 0:1, :], gm_ref[h, 1:2, :]), m_state)

    def body(c, m_prev):
        r0 = pl.multiple_of(N_META + c * A_CHUNK, BF16_SUBLANES)
        return run_chunk(pl.ds(r0, A_CHUNK), lambda h: (gc_ref[h, c, 0:1, :], gc_ref[h, c, 1:2, :]), m_prev)

    m_state = lax.fori_loop(0, n_chunks, body, m_state)
    for h in range(heads):
        m_out_ref[h] = jnp.broadcast_to(m_state[h], m_out_ref.shape[1:])


def _mlstm_prompt(u, g_meta, g_chunks, norm_g, *, dqk, dv, heads):
    bsz, t, _ = u.shape
    n_chunks = g_chunks.shape[2]
    groups = A_HEADS // heads
    v_base = 2 * A_HEADS * dqk // (heads * dv)
    tok = lambda width, base: pl.BlockSpec((None, t, heads * width), lambda b, g: (b, 0, base + g))
    hsp = lambda *shape: pl.BlockSpec((None, heads) + shape, lambda b, g: (b, g) + (0,) * len(shape))
    return pl.pallas_call(
        functools.partial(_mlstm_prompt_kernel, n_chunks=n_chunks, heads=heads),
        grid=(bsz, groups),
        in_specs=[tok(dqk, 0), tok(dqk, groups), tok(dv, v_base), tok(dv, v_base + groups),
                  tok(dv, v_base + 2 * groups),
                  hsp(2, N_META), hsp(n_chunks, 2, A_CHUNK),
                  pl.BlockSpec((heads, 1, dv), lambda b, g: (g, 0, 0))],
        out_specs=[pl.BlockSpec((None, t, heads * dv), lambda b, g: (b, 0, g)),
                   hsp(dqk, dv), hsp(1, dqk), hsp(1, LANES)],
        out_shape=[jax.ShapeDtypeStruct((bsz, t, A_HEADS * dv), BF16),
                   jax.ShapeDtypeStruct((bsz, A_HEADS, dqk, dv), F32),
                   jax.ShapeDtypeStruct((bsz, A_HEADS, 1, dqk), F32),
                   jax.ShapeDtypeStruct((bsz, A_HEADS, 1, LANES), F32)],
        compiler_params=_cparams("arbitrary", "arbitrary"),
        name="mlstm_prompt",
    )(u, u, u, u, u, g_meta, g_chunks, norm_g.reshape(A_HEADS, 1, dv))


def _mlstm_sample_kernel(u_ref, g_ref, ng_ref, c_ref, n_ref, m_ref, hg_ref, c_out_ref, n_out_ref, m_out_ref,
                         *, dqk, dv):
    s_len = hg_ref.shape[0]
    scale = dqk ** -0.5
    k_base = A_HEADS * dqk
    v_base = 2 * A_HEADS * dqk
    o_base = v_base + A_HEADS * dv
    z_base = o_base + A_HEADS * dv
    inputs = []
    for h in range(A_HEADS):
        inputs.append((
            u_ref[:, h * dqk:(h + 1) * dqk], u_ref[:, k_base + h * dqk:k_base + (h + 1) * dqk],
            u_ref[:, v_base + h * dv:v_base + (h + 1) * dv], u_ref[:, o_base + h * dv:o_base + (h + 1) * dv],
            u_ref[:, z_base + h * dv:z_base + (h + 1) * dv],
            g_ref[h:h + 1, :], g_ref[A_HEADS + h:A_HEADS + h + 1, :],
            c_ref[h], n_ref[h:h + 1, :], m_ref[h:h + 1, :], ng_ref[h:h + 1, :]))
    for h, (out, c_new, n_new, m_new) in enumerate(_mlstm_chunk_heads(inputs, scale)):
        hg_ref[:, h * dv:(h + 1) * dv] = out[:s_len]
        c_out_ref[h] = c_new
        n_out_ref[h:h + 1, :] = n_new
        m_out_ref[h:h + 1, :] = m_new


def _mlstm_sample(u_pad, g_rows, norm_g, c_prev, n_prev, m_prev, *, s_len, dqk, dv):
    dbs, length, width = u_pad.shape
    seq3 = lambda *shape: pl.BlockSpec((None,) + shape, lambda b: (b,) + (0,) * len(shape))
    return pl.pallas_call(
        functools.partial(_mlstm_sample_kernel, dqk=dqk, dv=dv),
        grid=(dbs,),
        in_specs=[seq3(length, width), seq3(2 * A_HEADS, length), pl.BlockSpec((A_HEADS, dv), lambda b: (0, 0)),
                  seq3(A_HEADS, dqk, dv), seq3(A_HEADS, dqk), seq3(A_HEADS, 1)],
        out_specs=[seq3(s_len, A_HEADS * dv), seq3(A_HEADS, dqk, dv), seq3(A_HEADS, dqk), seq3(A_HEADS, 1)],
        out_shape=[jax.ShapeDtypeStruct((dbs, s_len, A_HEADS * dv), F32),
                   jax.ShapeDtypeStruct((dbs, A_HEADS, dqk, dv), F32),
                   jax.ShapeDtypeStruct((dbs, A_HEADS, dqk), F32),
                   jax.ShapeDtypeStruct((dbs, A_HEADS, 1), F32)],
        compiler_params=_cparams("arbitrary"),
        name="mlstm_sample",
    )(u_pad, g_rows, norm_g, c_prev, n_prev, m_prev)


def _attn_prompt_kernel(q_ref, k_ref, v_ref, z_ref, o_ref, *, n_tiles):
    tq = q_ref.shape[0] // n_tiles
    row = lax.broadcasted_iota(jnp.int32, (tq, tq), 0)
    col = lax.broadcasted_iota(jnp.int32, (tq, tq), 1)
    for i in range(n_tiles):
        rows = pl.ds(i * tq, tq)
        q = q_ref[rows, :]
        m = jnp.full((tq, 1), -jnp.inf, F32)
        l = jnp.zeros((tq, 1), F32)
        acc = jnp.zeros((tq, v_ref.shape[1]), F32)
        for j in range(i + 1):
            keys = pl.ds(j * tq, tq)
            s = lax.dot_general(q, k_ref[keys, :], NT_DIMS, preferred_element_type=F32)
            if j == i:
                s = jnp.where(col <= row, s, -jnp.inf)
            m_new = jnp.maximum(m, jnp.max(s, axis=1, keepdims=True))
            alpha = jnp.exp(m - m_new)
            p = jnp.exp(s - m_new)
            l = alpha * l + jnp.sum(p, axis=1, keepdims=True)
            acc = alpha * acc + jnp.dot(p.astype(BF16), v_ref[keys, :], preferred_element_type=F32)
            m = m_new
        o_ref[rows, :] = ((acc / l) * _silu(z_ref[rows, :].astype(F32))).astype(o_ref.dtype)


def _attn_prompt(qcat, kcat, v, z, *, n_tiles):
    bsz, t, _ = qcat.shape
    qk = pl.BlockSpec((None, t, QK_HEAD_W), lambda b, h: (b, 0, h))
    hv = pl.BlockSpec((None, t, B_VDIM), lambda b, h: (b, 0, h))
    return pl.pallas_call(
        functools.partial(_attn_prompt_kernel, n_tiles=n_tiles),
        grid=(bsz, B_HEADS),
        in_specs=[qk, qk, hv, hv],
        out_specs=hv,
        out_shape=jax.ShapeDtypeStruct((bsz, t, B_HEADS * B_VDIM), BF16),
        compiler_params=_cparams("arbitrary", "arbitrary"),
        name="mla_attn_prompt",
    )(qcat, kcat, v, z)


def _attn_sample_kernel(pt_ref, ql_ref, qpe_ref, cn_ref, pn_ref, ckv_hbm, kpe_hbm, o_ref,
                        cpage, ppage, sem_c, sem_p, cbuf, pbuf, m_sc, l_sc, acc_sc, *, n_pages, n_groups, n_split):
    b = pl.program_id(0)
    last_seq = pl.num_programs(0) - 1
    per = n_pages // n_split
    width = per * PAGE_SIZE

    def page_copies(seq, grp, slot):
        copies = []
        for g in range(n_pages):
            page = pt_ref[seq, grp * n_pages + g]
            copies.append(pltpu.make_async_copy(ckv_hbm.at[page], cpage.at[slot, g], sem_c.at[slot]))
            copies.append(pltpu.make_async_copy(kpe_hbm.at[page], ppage.at[slot, g], sem_p.at[slot]))
        return copies

    @pl.when(b == 0)
    def _():
        for copy in page_copies(0, 0, 0):
            copy.start()

    m_sc[...] = jnp.full(m_sc.shape, -jnp.inf, F32)
    l_sc[...] = jnp.zeros(l_sc.shape, F32)
    acc_sc[...] = jnp.zeros(acc_sc.shape, F32)
    pbuf[B_ROPE:, :] = jnp.zeros((pbuf.shape[0] - B_ROPE, pbuf.shape[1]), BF16)
    ql = ql_ref[...]
    qpe = qpe_ref[...]

    def group(k, carry):
        slot = lax.rem(b * n_groups + k, 2)
        last_grp = k == n_groups - 1

        @pl.when(jnp.logical_not(jnp.logical_and(last_grp, b == last_seq)))
        def _():
            nxt_seq = jnp.where(last_grp, b + 1, b)
            nxt_grp = jnp.where(last_grp, 0, k + 1)
            for copy in page_copies(nxt_seq, nxt_grp, 1 - slot):
                copy.start()

        for copy in page_copies(b, k, slot):
            copy.wait()

        scores = []
        for part in range(n_split):
            for g in range(part * per, (part + 1) * per):
                keys = pl.ds(g * PAGE_SIZE, PAGE_SIZE)
                cbuf[keys, :] = cpage[slot, g].astype(BF16)
                pbuf[0:B_ROPE, keys] = ppage[slot, g].astype(BF16)
            keys = pl.ds(part * width, width)
            scores.append(lax.dot_general(ql, cbuf[keys, :], NT_DIMS, preferred_element_type=F32)
                          + jnp.dot(qpe, pbuf[:, keys], preferred_element_type=F32))
        partials = []
        for part, s in enumerate(scores):
            m_part = jnp.max(s, axis=1, keepdims=True)
            p = jnp.exp(s - m_part)
            partials.append((m_part, jnp.sum(p, axis=1, keepdims=True),
                             jnp.dot(p.astype(BF16), cbuf[pl.ds(part * width, width), :],
                                     preferred_element_type=F32)))

        m_prev = m_sc[...]
        m_new = m_prev
        for m_part, _, _ in partials:
            m_new = jnp.maximum(m_new, m_part)
        w_prev = jnp.exp(m_prev - m_new)
        l_new = w_prev * l_sc[...]
        acc = w_prev * acc_sc[...]
        for m_part, l_part, a_part in partials:
            w_part = jnp.exp(m_part - m_new)
            l_new = l_new + w_part * l_part
            acc = acc + w_part * a_part
        m_sc[...] = m_new
        l_sc[...] = l_new
        acc_sc[...] = acc
        return carry

    lax.fori_loop(0, n_groups, group, 0)

    n_rows = ql.shape[0]
    s_len = cn_ref.shape[0]
    tok = lax.broadcasted_iota(jnp.int32, (n_rows, 1), 0) // B_HEADS
    qlf = ql.astype(F32)
    qpf = qpe.astype(F32)[:, :B_ROPE]
    m_run, l_run, acc_run = m_sc[...], l_sc[...], acc_sc[...]
    for t in range(s_len):
        c_row = cn_ref[t:t + 1, :]
        sc = (jnp.sum(qlf * c_row, axis=1, keepdims=True)
              + jnp.sum(qpf * pn_ref[t:t + 1, :], axis=1, keepdims=True))
        sc = jnp.where(tok >= t, sc, -jnp.inf)
        m_nxt = jnp.maximum(m_run, sc)
        a = jnp.exp(m_run - m_nxt)
        pt = jnp.exp(sc - m_nxt)
        l_run = a * l_run + pt
        acc_run = a * acc_run + pt * c_row
        m_run = m_nxt
    o_ref[...] = (acc_run / l_run).astype(o_ref.dtype)


def _attn_sample(page_table, ql, qpe, c_new, pe_new, cache_ckv, cache_kpe_t, *, n_pages, n_split):
    dbs, n_rows, _ = ql.shape
    s_len = c_new.shape[1]
    n_groups = page_table.shape[1] // n_pages
    seq = lambda *shape: pl.BlockSpec((None,) + shape, lambda b, pt: (b,) + (0,) * len(shape))
    hbm = pl.BlockSpec(memory_space=pl.ANY)
    grid_spec = pltpu.PrefetchScalarGridSpec(
        num_scalar_prefetch=1,
        grid=(dbs,),
        in_specs=[seq(n_rows, KV_LORA), seq(n_rows, LANES), seq(s_len, KV_LORA), seq(s_len, B_ROPE), hbm, hbm],
        out_specs=pl.BlockSpec((None, n_rows, KV_LORA), lambda b, pt: (b, 0, 0)),
        scratch_shapes=[pltpu.VMEM((2, n_pages, PAGE_SIZE, KV_LORA), cache_ckv.dtype),
                        pltpu.VMEM((2, n_pages, B_ROPE, PAGE_SIZE), cache_kpe_t.dtype),
                        pltpu.SemaphoreType.DMA((2,)), pltpu.SemaphoreType.DMA((2,)),
                        pltpu.VMEM((n_pages * PAGE_SIZE, KV_LORA), BF16),
                        pltpu.VMEM((LANES, n_pages * PAGE_SIZE), BF16),
                        pltpu.VMEM((n_rows, 1), F32), pltpu.VMEM((n_rows, 1), F32),
                        pltpu.VMEM((n_rows, KV_LORA), F32)],
    )
    return pl.pallas_call(
        functools.partial(_attn_sample_kernel, n_pages=n_pages, n_groups=n_groups, n_split=n_split),
        grid_spec=grid_spec,
        out_shape=jax.ShapeDtypeStruct((dbs, n_rows, KV_LORA), BF16),
        compiler_params=_cparams("arbitrary"),
        name="mla_attn_sample",
    )(page_table, ql, qpe, c_new, pe_new, cache_ckv, cache_kpe_t)


def _rope_tables(pos):
    half = B_ROPE // 2
    inv = ROPE_BASE ** (-jnp.arange(half, dtype=F32) / half)
    ang = pos[:, None] * inv[None, :]
    cos, sin = jnp.cos(ang), jnp.sin(ang)
    zero = jnp.zeros_like(cos)
    return (jnp.concatenate([cos, cos, zero, zero], axis=1),
            jnp.concatenate([-sin, sin, zero, zero], axis=1))


def _dup_rope_cols(w):
    half = B_ROPE // 2
    x1, x2 = w[..., :half], w[..., half:]
    return jnp.concatenate([x1, x2, x2, x1], axis=-1)


def kernel(x_prompt, x_sample, state_mlstm_C, state_mlstm_n, state_mlstm_m, cache_ckv, cache_kpe, page_table,
           meta_tokens, a_w_in, a_b_gates, a_norm_g, a_w_out, a_ln_g, a_ln_b, kv_w_down, kv_norm_g, kv_w_up,
           b_w_in, b_q_norm_g, b_w_qb, b_w_out, b_ln_g, b_ln_b):
    bsz, seq, d_model = x_prompt.shape
    dbs, s_len, _ = x_sample.shape
    n_a, n_b = a_w_in.shape[0], b_w_in.shape[0]
    alpha = (2 * (n_a + n_b)) ** 0.25
    dqk = d_model // (2 * A_HEADS)
    dv = d_model // A_HEADS
    qk_w, v_w = A_HEADS * dqk, A_HEADS * dv
    main_w = 2 * qk_w + 3 * v_w
    t = seq + N_META
    n_chunks = seq // A_CHUNK
    past = page_table.shape[1] * PAGE_SIZE
    mp, ms = bsz * t, dbs * s_len

    n_tiles = 3 if t % (3 * BF16_SUBLANES) == 0 else 1
    bm_p = t // n_tiles
    bm_ln = _pick_block(mp, 384)
    bm_big = _pick_block(mp, 1376)
    bm_s = _pick_block(ms, 512)
    n_pages = min(32, page_table.shape[1])
    n_split = n_pages // 4 if n_pages % 4 == 0 else 1
    assert s_len <= SAMPLE_CHUNK and page_table.shape[1] % n_pages == 0

    w_up3 = kv_w_up.reshape(KV_LORA, B_HEADS, B_NOPE + B_VDIM)
    w_uk, w_uv = w_up3[..., :B_NOPE], w_up3[..., B_NOPE:]
    ng = B_HEADS // HEAD_GROUP
    w_kv_perm = jnp.concatenate(
        [w_uk.reshape(KV_LORA, ng, HEAD_GROUP * B_NOPE), w_uv.reshape(KV_LORA, ng, HEAD_GROUP * B_VDIM)],
        axis=2).reshape(KV_LORA, -1).astype(BF16)
    w_ukt = jnp.transpose(w_uk, (1, 2, 0)).astype(BF16)
    w_uvh = jnp.transpose(w_uv, (1, 0, 2)).astype(BF16)
    w_dn_ext = jnp.concatenate([kv_w_down[:, :KV_LORA], _dup_rope_cols(kv_w_down[:, KV_LORA:])], axis=1).astype(BF16)

    pos_p = jnp.arange(t, dtype=F32)
    pos_s = past + jnp.arange(s_len, dtype=F32)
    cs_p, sn_p = _rope_tables(pos_p)
    cs_s, sn_s = (jnp.tile(a, (bm_s // s_len, 1)) for a in _rope_tables(pos_s))

    meta = jnp.broadcast_to(meta_tokens.astype(x_prompt.dtype)[None], (bsz, N_META, d_model))
    xp = jnp.concatenate([meta, x_prompt], axis=1).reshape(mp, d_model)
    xs = x_sample.reshape(ms, d_model)
    xp16 = xs16 = None
    cache_kpe_t = jnp.swapaxes(cache_kpe, 1, 2)

    st_p, st_s = [], []
    for i in range(n_a):
        w_main = a_w_in[i][:, :main_w].astype(BF16)
        w_g = jnp.pad(a_w_in[i][:, main_w:], ((0, 0), (0, LANES - 2 * A_HEADS))).astype(BF16)
        b_g = jnp.pad(a_b_gates[i], (0, LANES - 2 * A_HEADS)).reshape(1, LANES)
        w_out = a_w_out[i].astype(BF16)

        xp_in = xp if xp16 is None else xp16
        xs_in = xs if xs16 is None else xs16
        u = _mm_plain(xp_in, w_main, BF16, bm=bm_p, bn=2048, name="a_in_prompt")
        gates = _mm_gates(xp_in, w_g, b_g, bm=bm_p, name="a_gates_prompt")
        g4 = gates[:, :2 * A_HEADS].reshape(bsz, t, 2, A_HEADS).transpose(0, 3, 2, 1)
        g_meta = g4[..., :N_META]
        g_chunks = g4[..., N_META:].reshape(bsz, A_HEADS, 2, n_chunks, A_CHUNK).transpose(0, 1, 3, 2, 4)
        hg, c_p, n_p, m_p = _mlstm_prompt(u.reshape(bsz, t, main_w), g_meta, g_chunks, a_norm_g[i], dqk=dqk, dv=dv,
                                          heads=MLSTM_HEAD_GROUP)
        xp, xp16 = _mm_resid_ln(hg.reshape(mp, v_w), w_out, xp, a_ln_g[i], a_ln_b[i], alpha, (F32, BF16),
                                bm=bm_ln, name="a_out_prompt")
        st_p.append((c_p, n_p.reshape(bsz, A_HEADS, dqk), m_p[:, :, 0, 0]))

        us = _mm_plain(xs_in, w_main, BF16, bm=bm_s, bn=1024, name="a_in_sample").reshape(dbs, s_len, main_w)
        u_pad = jnp.pad(us, ((0, 0), (0, SAMPLE_CHUNK - s_len), (0, 0)))
        gs = _mm_gates(xs_in, w_g, b_g, bm=bm_s, name="a_gates_sample")
        g3 = gs[:, :2 * A_HEADS].reshape(dbs, s_len, 2 * A_HEADS).transpose(0, 2, 1)
        tail = ((0, 0), (0, 0), (0, SAMPLE_CHUNK - s_len))
        g_rows = jnp.concatenate([jnp.pad(g3[:, :A_HEADS], tail, constant_values=-jnp.inf),
                                  jnp.pad(g3[:, A_HEADS:], tail)], axis=1)
        hgs, c_s, n_s, m_s = _mlstm_sample(u_pad, g_rows, a_norm_g[i], state_mlstm_C[:, i], state_mlstm_n[:, i],
                                           state_mlstm_m[:, i][..., None], s_len=s_len, dqk=dqk, dv=dv)
        xs, xs16 = _mm_resid_ln(hgs.reshape(ms, v_w), w_out, xs, a_ln_g[i], a_ln_b[i], alpha, (F32, BF16),
                                bm=bm_s, name="a_out_sample")
        st_s.append((c_s, n_s, m_s[..., 0]))

    if xp16 is None:
        xp16, xs16 = xp.astype(BF16), xs.astype(BF16)
    for j in range(n_b):
        outs = (F32,) if j == n_b - 1 else (F32, BF16)
        if j == 0:
            ckv_p, kpe_p, ckv16_p, kpe16_p = _mm_kv_down(xp16, w_dn_ext, kv_norm_g, cs_p, sn_p, bm=bm_p,
                                                         name="kv_down_prompt")
            ckv_s, kpe_s, _, _ = _mm_kv_down(xs16, w_dn_ext, kv_norm_g, cs_s, sn_s, bm=bm_s, name="kv_down_sample")
            kcat, vcat = _mm_kv_up(ckv16_p, w_kv_perm, kpe16_p, bm=bm_p, name="kv_up_prompt")
        w_ql = b_w_in[j][:, :Q_LORA].astype(BF16)
        w_z = b_w_in[j][:, Q_LORA:].astype(BF16)
        w_qb3 = b_w_qb[j].reshape(Q_LORA, B_HEADS, B_NOPE + B_ROPE)
        w_q_perm = jnp.concatenate([w_qb3[..., :B_NOPE], _dup_rope_cols(w_qb3[..., B_NOPE:])],
                                   axis=2).reshape(Q_LORA, -1).astype(BF16)
        w_out = b_w_out[j].astype(BF16)

        qlat = _mm_rms(xp16, w_ql, b_q_norm_g[j], bm=bm_big, name="b_qlat_prompt")
        z = _mm_plain(xp16, w_z, BF16, bm=bm_big, bn=1024, name="b_z_prompt")
        qcat = _mm_q_up(qlat, w_q_perm, cs_p, sn_p, bm=bm_p, name="b_q_up_prompt")
        og = _attn_prompt(qcat.reshape(bsz, t, -1), kcat.reshape(bsz, t, -1), vcat.reshape(bsz, t, -1),
                          z.reshape(bsz, t, -1), n_tiles=n_tiles)
        xp, *rest = _mm_resid_ln(og.reshape(mp, -1), w_out, xp, b_ln_g[j], b_ln_b[j], alpha, outs, bm=bm_ln,
                                 name="b_out_prompt")
        xp16 = rest[0] if rest else None

        qlat_s = _mm_rms(xs16, w_ql, b_q_norm_g[j], bm=bm_s, name="b_qlat_sample")
        z_s = _mm_plain(xs16, w_z, BF16, bm=bm_s, bn=1024, name="b_z_sample")
        qcat_s = _mm_q_up(qlat_s, w_q_perm, cs_s, sn_s, bm=bm_s, name="b_q_up_sample")
        ql = _mm_q_absorb(qcat_s, w_ukt, name="b_q_absorb_sample")
        qpe = qcat_s.reshape(ms, B_HEADS, QK_HEAD_W)[:, :, B_NOPE:]
        olat = _attn_sample(page_table, ql.reshape(dbs, s_len * B_HEADS, KV_LORA),
                            qpe.reshape(dbs, s_len * B_HEADS, LANES),
                            ckv_s.reshape(dbs, s_len, KV_LORA), kpe_s.reshape(dbs, s_len, B_ROPE),
                            cache_ckv, cache_kpe_t, n_pages=n_pages, n_split=n_split)
        ogs = _mm_o_up(olat.reshape(ms, B_HEADS * KV_LORA), w_uvh, z_s, name="b_o_up_sample")
        xs, *rest = _mm_resid_ln(ogs, w_out, xs, b_ln_g[j], b_ln_b[j], alpha, outs, bm=bm_s, name="b_out_sample")
        xs16 = rest[0] if rest else None

    stack = lambda parts, k: jnp.stack([p[k] for p in parts], axis=1)
    return (xp.reshape(bsz, t, d_model)[:, N_META:], xs.reshape(dbs, s_len, d_model),
            stack(st_p, 0), stack(st_p, 1), stack(st_p, 2),
            ckv_p.reshape(bsz, t, KV_LORA), kpe_p.reshape(bsz, t, B_ROPE),
            stack(st_s, 0), stack(st_s, 1), stack(st_s, 2),
            ckv_s.reshape(dbs, s_len, KV_LORA), kpe_s.reshape(dbs, s_len, B_ROPE))
```

```python
import functools

import jax
import jax.numpy as jnp
from jax import lax
from jax.experimental import pallas as pl
from jax.experimental.pallas import tpu as pltpu

F32 = jnp.float32
BF16 = jnp.bfloat16

N_META = 16
A_HEADS = 8
A_CHUNK = 64
IGATE_CAP = 15.0
B_HEADS = 16
B_NOPE = 128
B_ROPE = 64
B_VDIM = 128
KV_LORA = 512
Q_LORA = 512
PAGE_SIZE = 128
ROPE_BASE = 10000.0
LN_EPS = 1e-5
RMS_EPS = 1e-6
B_SCALE = (B_NOPE + B_ROPE) ** -0.5

LANES = 128
BF16_SUBLANES = 16
VMEM_LIMIT_BYTES = 56 * 1024 * 1024
HEAD_GROUP = 4
MLSTM_HEAD_GROUP = 4
SAMPLE_CHUNK = BF16_SUBLANES
QK_HEAD_W = 2 * LANES

NT_DIMS = (((1,), (1,)), ((), ()))
TN_DIMS = (((0,), (0,)), ((), ()))


def _cparams(*sem):
    return pltpu.CompilerParams(dimension_semantics=sem, vmem_limit_bytes=VMEM_LIMIT_BYTES)


def _pick_block(m, target, mult=BF16_SUBLANES):
    best = None
    for d in range(mult, min(m, target) + 1, mult):
        if m % d == 0:
            best = d
    return best if best is not None else m


def _layer_norm(y, g, b):
    mu = jnp.mean(y, axis=-1, keepdims=True)
    yc = y - mu
    var = jnp.mean(yc * yc, axis=-1, keepdims=True)
    return yc * lax.rsqrt(var + LN_EPS) * g + b


def _rms_norm(y, g):
    return y * lax.rsqrt(jnp.mean(y * y, axis=-1, keepdims=True) + RMS_EPS) * g


def _rope_dup(v, cs, sn):
    return v * cs + pltpu.roll(v, 2 * (B_ROPE // 2), axis=1) * sn


def _silu(z):
    return z * jax.nn.sigmoid(z)


def _mm_plain_kernel(x_ref, w_ref, o_ref):
    acc = jnp.dot(x_ref[...].astype(BF16), w_ref[...], preferred_element_type=F32)
    o_ref[...] = acc.astype(o_ref.dtype)


def _mm_plain(x, w, out_dtype, *, bm, bn, name):
    m, k = x.shape
    n = w.shape[1]
    return pl.pallas_call(
        _mm_plain_kernel,
        grid=(n // bn, m // bm),
        in_specs=[pl.BlockSpec((bm, k), lambda j, i: (i, 0)),
                  pl.BlockSpec((k, bn), lambda j, i: (0, j))],
        out_specs=pl.BlockSpec((bm, bn), lambda j, i: (i, j)),
        out_shape=jax.ShapeDtypeStruct((m, n), out_dtype),
        compiler_params=_cparams("arbitrary", "arbitrary"),
        name=name,
    )(x, w)


def _mm_gates_kernel(x_ref, w_ref, b_ref, o_ref):
    a = jnp.dot(x_ref[...].astype(BF16), w_ref[...], preferred_element_type=F32) + b_ref[...]
    lane = lax.broadcasted_iota(jnp.int32, a.shape, 1)
    gi = IGATE_CAP * jnp.tanh(a / IGATE_CAP)
    lf = jnp.minimum(a, 0.0) - jnp.log1p(jnp.exp(-jnp.abs(a)))
    o_ref[...] = jnp.where(lane < A_HEADS, gi, lf)


def _mm_gates(x, w, b, *, bm, name):
    m, k = x.shape
    return pl.pallas_call(
        _mm_gates_kernel,
        grid=(m // bm,),
        in_specs=[pl.BlockSpec((bm, k), lambda i: (i, 0)),
                  pl.BlockSpec((k, LANES), lambda i: (0, 0)),
                  pl.BlockSpec((1, LANES), lambda i: (0, 0))],
        out_specs=pl.BlockSpec((bm, LANES), lambda i: (i, 0)),
        out_shape=jax.ShapeDtypeStruct((m, LANES), F32),
        compiler_params=_cparams("arbitrary"),
        name=name,
    )(x, w, b)


def _mm_resid_ln_kernel(x_ref, w_ref, r_ref, g_ref, b_ref, *o_refs, alpha):
    acc = jnp.dot(x_ref[...].astype(BF16), w_ref[...], preferred_element_type=F32)
    y = _layer_norm(alpha * r_ref[...] + acc, g_ref[...], b_ref[...])
    for o_ref in o_refs:
        o_ref[...] = y.astype(o_ref.dtype)


def _mm_resid_ln(x, w, resid, g, b, alpha, out_dtypes, *, bm, name):
    m, k = x.shape
    n = w.shape[1]
    row = lambda i: (i, 0)
    fixed = lambda i: (0, 0)
    outs = pl.pallas_call(
        functools.partial(_mm_resid_ln_kernel, alpha=alpha),
        grid=(m // bm,),
        in_specs=[pl.BlockSpec((bm, k), row), pl.BlockSpec((k, n), fixed),
                  pl.BlockSpec((bm, n), row), pl.BlockSpec((1, n), fixed), pl.BlockSpec((1, n), fixed)],
        out_specs=[pl.BlockSpec((bm, n), row) for _ in out_dtypes],
        out_shape=[jax.ShapeDtypeStruct((m, n), dt) for dt in out_dtypes],
        compiler_params=_cparams("arbitrary"),
        name=name,
    )(x, w, resid, g.reshape(1, n), b.reshape(1, n))
    return outs


def _mm_mlstm_out_kernel(h_ref, o_ref, z_ref, ng_ref, w_ref, r_ref, g_ref, b_ref, *out_refs, alpha, heads):
    dv = h_ref.shape[1] // heads
    parts = []
    for hd in range(heads):
        cols = slice(hd * dv, (hd + 1) * dv)
        h = h_ref[:, cols].astype(F32)
        hn = h * lax.rsqrt(jnp.mean(h * h, axis=1, keepdims=True) + RMS_EPS) * ng_ref[:, cols]
        gate = jax.nn.sigmoid(o_ref[:, cols].astype(F32)) * _silu(z_ref[:, cols].astype(F32))
        parts.append((hn * gate).astype(BF16))
    acc = jnp.dot(jnp.concatenate(parts, axis=1), w_ref[...], preferred_element_type=F32)
    y = _layer_norm(alpha * r_ref[...] + acc, g_ref[...], b_ref[...])
    for out_ref in out_refs:
        out_ref[...] = y.astype(out_ref.dtype)


def _mm_mlstm_out(h, u, norm_g, w, resid, g, b, alpha, out_dtypes, *, heads, bm, name):
    m, k = h.shape
    n = w.shape[1]
    o_blk = (u.shape[1] - 2 * k) // k
    row = lambda i: (i, 0)
    fixed = lambda i: (0, 0)
    return pl.pallas_call(
        functools.partial(_mm_mlstm_out_kernel, alpha=alpha, heads=heads),
        grid=(m // bm,),
        in_specs=[pl.BlockSpec((bm, k), row), pl.BlockSpec((bm, k), lambda i: (i, o_blk)),
                  pl.BlockSpec((bm, k), lambda i: (i, o_blk + 1)), pl.BlockSpec((1, k), fixed),
                  pl.BlockSpec((k, n), fixed), pl.BlockSpec((bm, n), row),
                  pl.BlockSpec((1, n), fixed), pl.BlockSpec((1, n), fixed)],
        out_specs=[pl.BlockSpec((bm, n), row) for _ in out_dtypes],
        out_shape=[jax.ShapeDtypeStruct((m, n), dt) for dt in out_dtypes],
        compiler_params=_cparams("arbitrary"),
        name=name,
    )(h, u, u, norm_g.reshape(1, k), w, resid, g.reshape(1, n), b.reshape(1, n))


def _mm_resid_ln_tail_kernel(x_ref, w_ref, r_ref, g_ref, b_ref, o_ref, *, alpha):
    acc = jnp.dot(x_ref[0].astype(BF16), w_ref[...], preferred_element_type=F32)
    o_ref[...] = _layer_norm(alpha * r_ref[0] + acc, g_ref[...], b_ref[...])


def _mm_resid_ln_tail(x, w, resid, g, b, alpha, *, skip, bm, name):
    bsz, t, k = x.shape
    n = w.shape[1]
    rows = t - skip
    tail = lambda width: pl.BlockSpec((pl.Element(1), pl.Element(bm), pl.Element(width)),
                                      lambda bi, i: (bi, pl.multiple_of(skip + i * bm, BF16_SUBLANES), 0))
    fixed = lambda bi, i: (0, 0)
    return pl.pallas_call(
        functools.partial(_mm_resid_ln_tail_kernel, alpha=alpha),
        grid=(bsz, rows // bm),
        in_specs=[tail(k), pl.BlockSpec((k, n), fixed), tail(n),
                  pl.BlockSpec((1, n), fixed), pl.BlockSpec((1, n), fixed)],
        out_specs=[pl.BlockSpec((None, bm, n), lambda bi, i: (bi, i, 0))],
        out_shape=[jax.ShapeDtypeStruct((bsz, rows, n), F32)],
        compiler_params=_cparams("arbitrary", "arbitrary"),
        name=name,
    )(x, w, resid, g.reshape(1, n), b.reshape(1, n))[0]


def _mm_kv_down_kernel(x_ref, w_ref, g_ref, cs_ref, sn_ref, ckv_ref, kpe_ref, ckv16_ref, kpe16_ref):
    acc = jnp.dot(x_ref[...].astype(BF16), w_ref[...], preferred_element_type=F32)
    ckv = _rms_norm(acc[:, :KV_LORA], g_ref[...])
    pe = _rope_dup(acc[:, KV_LORA:], cs_ref[...], sn_ref[...])
    ckv_ref[...] = ckv
    ckv16_ref[...] = ckv.astype(BF16)
    kpe_ref[...] = pe[:, :B_ROPE]
    kpe16_ref[...] = pe.astype(BF16)


def _mm_kv_down(x, w_ext, g, cs, sn, *, bm, name):
    m, k = x.shape
    n = w_ext.shape[1]
    nt = cs.shape[0] // bm
    row = lambda i: (i, 0)
    fixed = lambda i: (0, 0)
    tab = lambda i: (i % nt, 0)
    return pl.pallas_call(
        _mm_kv_down_kernel,
        grid=(m // bm,),
        in_specs=[pl.BlockSpec((bm, k), row), pl.BlockSpec((k, n), fixed), pl.BlockSpec((1, KV_LORA), fixed),
                  pl.BlockSpec((bm, LANES), tab), pl.BlockSpec((bm, LANES), tab)],
        out_specs=[pl.BlockSpec((bm, KV_LORA), row), pl.BlockSpec((bm, B_ROPE), row),
                   pl.BlockSpec((bm, KV_LORA), row), pl.BlockSpec((bm, LANES), row)],
        out_shape=[jax.ShapeDtypeStruct((m, KV_LORA), F32), jax.ShapeDtypeStruct((m, B_ROPE), F32),
                   jax.ShapeDtypeStruct((m, KV_LORA), BF16), jax.ShapeDtypeStruct((m, LANES), BF16)],
        compiler_params=_cparams("arbitrary"),
        name=name,
    )(x, w_ext, g.reshape(1, KV_LORA), cs, sn)


def _mm_kv_up_kernel(c_ref, w_ref, kpe_ref, k_ref, v_ref):
    acc = jnp.dot(c_ref[...], w_ref[...], preferred_element_type=F32)
    kpe = kpe_ref[...]
    for h in range(HEAD_GROUP):
        k_ref[:, h * QK_HEAD_W:h * QK_HEAD_W + B_NOPE] = acc[:, h * B_NOPE:(h + 1) * B_NOPE].astype(BF16)
        k_ref[:, h * QK_HEAD_W + B_NOPE:(h + 1) * QK_HEAD_W] = kpe
    v_ref[...] = acc[:, HEAD_GROUP * B_NOPE:].astype(BF16)


def _mm_kv_up(ckv16, w_perm, kpe16, *, bm, name):
    m, k = ckv16.shape
    ng = B_HEADS // HEAD_GROUP
    wn = HEAD_GROUP * (B_NOPE + B_VDIM)
    return pl.pallas_call(
        _mm_kv_up_kernel,
        grid=(ng, m // bm),
        in_specs=[pl.BlockSpec((bm, k), lambda j, i: (i, 0)),
                  pl.BlockSpec((k, wn), lambda j, i: (0, j)),
                  pl.BlockSpec((bm, LANES), lambda j, i: (i, 0))],
        out_specs=[pl.BlockSpec((bm, HEAD_GROUP * QK_HEAD_W), lambda j, i: (i, j)),
                   pl.BlockSpec((bm, HEAD_GROUP * B_VDIM), lambda j, i: (i, j))],
        out_shape=[jax.ShapeDtypeStruct((m, B_HEADS * QK_HEAD_W), BF16),
                   jax.ShapeDtypeStruct((m, B_HEADS * B_VDIM), BF16)],
        compiler_params=_cparams("arbitrary", "arbitrary"),
        name=name,
    )(ckv16, w_perm, kpe16)


def _mm_rms_kernel(x_ref, w_ref, g_ref, o_ref):
    acc = jnp.dot(x_ref[...].astype(BF16), w_ref[...], preferred_element_type=F32)
    o_ref[...] = _rms_norm(acc, g_ref[...]).astype(o_ref.dtype)


def _mm_rms(x, w, g, *, bm, name):
    m, k = x.shape
    n = w.shape[1]
    return pl.pallas_call(
        _mm_rms_kernel,
        grid=(m // bm,),
        in_specs=[pl.BlockSpec((bm, k), lambda i: (i, 0)), pl.BlockSpec((k, n), lambda i: (0, 0)),
                  pl.BlockSpec((1, n), lambda i: (0, 0))],
        out_specs=pl.BlockSpec((bm, n), lambda i: (i, 0)),
        out_shape=jax.ShapeDtypeStruct((m, n), BF16),
        compiler_params=_cparams("arbitrary"),
        name=name,
    )(x, w, g.reshape(1, n))


def _mm_q_up_kernel(x_ref, w_ref, cs_ref, sn_ref, q_ref):
    acc = jnp.dot(x_ref[...], w_ref[...], preferred_element_type=F32) * B_SCALE
    cs = cs_ref[...]
    sn = sn_ref[...]
    for h in range(HEAD_GROUP):
        lo = h * QK_HEAD_W
        q_ref[:, lo:lo + B_NOPE] = acc[:, lo:lo + B_NOPE].astype(BF16)
        q_ref[:, lo + B_NOPE:lo + QK_HEAD_W] = _rope_dup(acc[:, lo + B_NOPE:lo + QK_HEAD_W], cs, sn).astype(BF16)


def _mm_q_up(qlat, w_perm, cs, sn, *, bm, name):
    m, k = qlat.shape
    ng = B_HEADS // HEAD_GROUP
    wn = HEAD_GROUP * QK_HEAD_W
    nt = cs.shape[0] // bm
    return pl.pallas_call(
        _mm_q_up_kernel,
        grid=(ng, m // bm),
        in_specs=[pl.BlockSpec((bm, k), lambda j, i: (i, 0)),
                  pl.BlockSpec((k, wn), lambda j, i: (0, j)),
                  pl.BlockSpec((bm, LANES), lambda j, i: (i % nt, 0)),
                  pl.BlockSpec((bm, LANES), lambda j, i: (i % nt, 0))],
        out_specs=pl.BlockSpec((bm, wn), lambda j, i: (i, j)),
        out_shape=jax.ShapeDtypeStruct((m, B_HEADS * QK_HEAD_W), BF16),
        compiler_params=_cparams("arbitrary", "arbitrary"),
        name=name,
    )(qlat, w_perm, cs, sn)


def _mm_heads_kernel(x_ref, w_ref, o_ref):
    o_ref[...] = jnp.dot(x_ref[...].astype(BF16), w_ref[...], preferred_element_type=F32).astype(o_ref.dtype)


def _mm_q_absorb(qcat, w_ukt, *, name):
    m = qcat.shape[0]
    return pl.pallas_call(
        _mm_heads_kernel,
        grid=(B_HEADS,),
        in_specs=[pl.BlockSpec((m, B_NOPE), lambda h: (0, 2 * h)),
                  pl.BlockSpec((None, B_NOPE, KV_LORA), lambda h: (h, 0, 0))],
        out_specs=pl.BlockSpec((m, KV_LORA), lambda h: (0, h)),
        out_shape=jax.ShapeDtypeStruct((m, B_HEADS * KV_LORA), BF16),
        compiler_params=_cparams("arbitrary"),
        name=name,
    )(qcat, w_ukt)


def _mm_o_up_kernel(x_ref, w_ref, z_ref, o_ref):
    acc = jnp.dot(x_ref[...].astype(BF16), w_ref[...], preferred_element_type=F32)
    o_ref[...] = (acc * _silu(z_ref[...].astype(F32))).astype(o_ref.dtype)


def _mm_o_up(olat, w_uv, z, *, name):
    m = olat.shape[0]
    return pl.pallas_call(
        _mm_o_up_kernel,
        grid=(B_HEADS,),
        in_specs=[pl.BlockSpec((m, KV_LORA), lambda h: (0, h)),
                  pl.BlockSpec((None, KV_LORA, B_VDIM), lambda h: (h, 0, 0)),
                  pl.BlockSpec((m, B_VDIM), lambda h: (0, h))],
        out_specs=pl.BlockSpec((m, B_VDIM), lambda h: (0, h)),
        out_shape=jax.ShapeDtypeStruct((m, B_HEADS * B_VDIM), BF16),
        compiler_params=_cparams("arbitrary"),
        name=name,
    )(olat, w_uv, z)


def _mlstm_gate_terms(lf_row, gi_row, m_prev):
    length = lf_row.shape[1]
    t_idx = lax.broadcasted_iota(jnp.int32, (length, length), 0)
    s_idx = lax.broadcasted_iota(jnp.int32, (length, length), 1)
    causal = s_idx <= t_idx
    eye = s_idx == t_idx
    b_col = jnp.sum(jnp.where(causal, lf_row, 0.0), axis=1, keepdims=True)
    b_row = jnp.sum(jnp.where(eye, b_col, 0.0), axis=0, keepdims=True)
    gi_col = jnp.sum(jnp.where(eye, gi_row, 0.0), axis=1, keepdims=True)
    log_d = jnp.where(causal, b_col - b_row + gi_row, -jnp.inf)
    m_t = jnp.maximum(jnp.max(log_d, axis=1, keepdims=True), b_col + m_prev)
    w_inter = jnp.exp(b_col + m_prev - m_t)
    d = jnp.exp(log_d - m_t)
    b_end = b_row[:, length - 1:length]
    log_w_row = b_end - b_row + gi_row
    m_new = jnp.maximum(b_end + m_prev, jnp.max(log_w_row, axis=1, keepdims=True))
    w_c = jnp.exp(b_end + m_prev - m_new)
    w_s_col = jnp.exp(b_end - b_col + gi_col - m_new)
    return d, m_t, w_inter, w_s_col, w_c, m_new


def _mlstm_chunk_heads(heads, scale):
    stage_a = []
    for q, k, v, gi_row, lf_row, c_prev, n_prev, m_prev in heads:
        terms = _mlstm_gate_terms(lf_row, gi_row, m_prev)
        qk = lax.dot_general(q, k, NT_DIMS, preferred_element_type=F32)
        q_c = jnp.dot(q, c_prev.astype(BF16), preferred_element_type=F32)
        stage_a.append((terms, qk, q_c))
    stage_b = []
    for (q, k, v, _, _, c_prev, n_prev, _), (terms, qk, q_c) in zip(heads, stage_a):
        d, m_t, w_inter, w_s_col, w_c, m_new = terms
        s = qk * (scale * d)
        s_v = jnp.dot(s.astype(BF16), v, preferred_element_type=F32)
        w_v = (w_s_col * v.astype(F32)).astype(BF16)
        k_v = lax.dot_general(k, w_v, TN_DIMS, preferred_element_type=F32)
        stage_b.append((s, s_v, k_v))
    results = []
    for (q, k, v, _, _, c_prev, n_prev, _), (terms, _, q_c), (s, s_v, k_v) in zip(heads, stage_a, stage_b):
        d, m_t, w_inter, w_s_col, w_c, m_new = terms
        w_q = w_inter * scale
        num = w_q * q_c + s_v
        q_n = jnp.sum(q.astype(F32) * n_prev, axis=1, keepdims=True)
        den = w_q * q_n + jnp.sum(s, axis=1, keepdims=True)
        h = num / jnp.maximum(jnp.abs(den), jnp.exp(-m_t))
        c_new = w_c * c_prev + k_v
        n_new = w_c * n_prev + jnp.sum(w_s_col * k.astype(F32), axis=0, keepdims=True)
        results.append((h, c_new, n_new, m_new))
    return results


def _mlstm_prompt_kernel(q_ref, k_ref, v_ref, gm_ref, gc_ref, h_ref, c_ref, n_ref, m_out_ref, *, n_chunks, heads):
    dqk = q_ref.shape[1] // heads
    dv = v_ref.shape[1] // heads
    scale = dqk ** -0.5

    def run_chunk(rows, gates, m_prev):
        inputs = []
        for h in range(heads):
            qk_cols = slice(h * dqk, (h + 1) * dqk)
            v_cols = slice(h * dv, (h + 1) * dv)
            inputs.append((q_ref[rows, qk_cols], k_ref[rows, qk_cols], v_ref[rows, v_cols], *gates(h),
                           c_ref[h], n_ref[h], m_prev[h]))
        m_next = []
        for h, (out, c_new, n_new, m_new) in enumerate(_mlstm_chunk_heads(inputs, scale)):
            h_ref[rows, h * dv:(h + 1) * dv] = out.astype(h_ref.dtype)
            c_ref[h] = c_new
            n_ref[h] = n_new
            m_next.append(m_new)
        return tuple(m_next)

    c_ref[...] = jnp.zeros(c_ref.shape, F32)
    n_ref[...] = jnp.zeros(n_ref.shape, F32)
    m_state = tuple(jnp.zeros((1, 1), F32) for _ in range(heads))
    m_state = run_chunk(pl.ds(0, N_META), lambda h: (gm_ref[h, 0:1, :], gm_ref[h, 1:2, :]), m_state)

    def body(c, m_prev):
        r0 = pl.multiple_of(N_META + c * A_CHUNK, BF16_SUBLANES)
        return run_chunk(pl.ds(r0, A_CHUNK), lambda h: (gc_ref[h, c, 0:1, :], gc_ref[h, c, 1:2, :]), m_prev)

    m_state = lax.fori_loop(0, n_chunks, body, m_state)
    for h in range(heads):
        m_out_ref[h] = jnp.broadcast_to(m_state[h], m_out_ref.shape[1:])


def _mlstm_prompt(u, g_meta, g_chunks, *, dqk, dv, heads):
    bsz, t, _ = u.shape
    n_chunks = g_chunks.shape[2]
    groups = A_HEADS // heads
    v_base = 2 * A_HEADS * dqk // (heads * dv)
    tok = lambda width, base: pl.BlockSpec((None, t, heads * width), lambda b, g: (b, 0, base + g))
    hsp = lambda *shape: pl.BlockSpec((None, heads) + shape, lambda b, g: (b, g) + (0,) * len(shape))
    return pl.pallas_call(
        functools.partial(_mlstm_prompt_kernel, n_chunks=n_chunks, heads=heads),
        grid=(bsz, groups),
        in_specs=[tok(dqk, 0), tok(dqk, groups), tok(dv, v_base), hsp(2, N_META), hsp(n_chunks, 2, A_CHUNK)],
        out_specs=[pl.BlockSpec((None, t, heads * dv), lambda b, g: (b, 0, g)),
                   hsp(dqk, dv), hsp(1, dqk), hsp(1, LANES)],
        out_shape=[jax.ShapeDtypeStruct((bsz, t, A_HEADS * dv), BF16),
                   jax.ShapeDtypeStruct((bsz, A_HEADS, dqk, dv), F32),
                   jax.ShapeDtypeStruct((bsz, A_HEADS, 1, dqk), F32),
                   jax.ShapeDtypeStruct((bsz, A_HEADS, 1, LANES), F32)],
        compiler_params=_cparams("arbitrary", "arbitrary"),
        name="mlstm_prompt",
    )(u, u, u, g_meta, g_chunks)


def _mlstm_sample_kernel(u_ref, g_ref, c_ref, n_ref, m_ref, h_ref, c_out_ref, n_out_ref, m_out_ref, *, dqk, dv):
    n_seq, s_len = h_ref.shape[0], h_ref.shape[1]
    scale = dqk ** -0.5
    k_base = A_HEADS * dqk
    v_base = 2 * A_HEADS * dqk
    inputs = []
    for i in range(n_seq):
        for h in range(A_HEADS):
            inputs.append((
                u_ref[i, :, h * dqk:(h + 1) * dqk], u_ref[i, :, k_base + h * dqk:k_base + (h + 1) * dqk],
                u_ref[i, :, v_base + h * dv:v_base + (h + 1) * dv],
                g_ref[i, h:h + 1, :], g_ref[i, A_HEADS + h:A_HEADS + h + 1, :],
                c_ref[i, h], n_ref[i, h:h + 1, :], m_ref[i, h:h + 1, :]))
    for idx, (out, c_new, n_new, m_new) in enumerate(_mlstm_chunk_heads(inputs, scale)):
        i, h = divmod(idx, A_HEADS)
        h_ref[i, :, h * dv:(h + 1) * dv] = out[:s_len]
        c_out_ref[i, h] = c_new
        n_out_ref[i, h:h + 1, :] = n_new
        m_out_ref[i, h:h + 1, :] = m_new


def _mlstm_sample(u_pad, g_rows, c_prev, n_prev, m_prev, *, s_len, n_seq, dqk, dv):
    dbs, length, width = u_pad.shape
    seq3 = lambda *shape: pl.BlockSpec((n_seq,) + shape, lambda b: (b,) + (0,) * len(shape))
    return pl.pallas_call(
        functools.partial(_mlstm_sample_kernel, dqk=dqk, dv=dv),
        grid=(dbs // n_seq,),
        in_specs=[seq3(length, width), seq3(2 * A_HEADS, length),
                  seq3(A_HEADS, dqk, dv), seq3(A_HEADS, dqk), seq3(A_HEADS, 1)],
        out_specs=[seq3(s_len, A_HEADS * dv), seq3(A_HEADS, dqk, dv), seq3(A_HEADS, dqk), seq3(A_HEADS, 1)],
        out_shape=[jax.ShapeDtypeStruct((dbs, s_len, A_HEADS * dv), F32),
                   jax.ShapeDtypeStruct((dbs, A_HEADS, dqk, dv), F32),
                   jax.ShapeDtypeStruct((dbs, A_HEADS, dqk), F32),
                   jax.ShapeDtypeStruct((dbs, A_HEADS, 1), F32)],
        compiler_params=_cparams("arbitrary"),
        name="mlstm_sample",
    )(u_pad, g_rows, c_prev, n_prev, m_prev)


def _attn_prompt_kernel(q_ref, k_ref, v_ref, z_ref, o_ref, *, n_tiles):
    tq = q_ref.shape[0] // n_tiles
    row = lax.broadcasted_iota(jnp.int32, (tq, tq), 0)
    col = lax.broadcasted_iota(jnp.int32, (tq, tq), 1)
    for i in range(n_tiles):
        rows = pl.ds(i * tq, tq)
        q = q_ref[rows, :]
        m = jnp.full((tq, 1), -jnp.inf, F32)
        l = jnp.zeros((tq, 1), F32)
        acc = jnp.zeros((tq, v_ref.shape[1]), F32)
        for j in range(i + 1):
            keys = pl.ds(j * tq, tq)
            s = lax.dot_general(q, k_ref[keys, :], NT_DIMS, preferred_element_type=F32)
            if j == i:
                s = jnp.where(col <= row, s, -jnp.inf)
            m_new = jnp.maximum(m, jnp.max(s, axis=1, keepdims=True))
            alpha = jnp.exp(m - m_new)
            p = jnp.exp(s - m_new)
            l = alpha * l + jnp.sum(p, axis=1, keepdims=True)
            acc = alpha * acc + jnp.dot(p.astype(BF16), v_ref[keys, :], preferred_element_type=F32)
            m = m_new
        o_ref[rows, :] = ((acc / l) * _silu(z_ref[rows, :].astype(F32))).astype(o_ref.dtype)


def _attn_prompt(qcat, kcat, v, z, *, n_tiles):
    bsz, t, _ = qcat.shape
    qk = pl.BlockSpec((None, t, QK_HEAD_W), lambda b, h: (b, 0, h))
    hv = pl.BlockSpec((None, t, B_VDIM), lambda b, h: (b, 0, h))
    return pl.pallas_call(
        functools.partial(_attn_prompt_kernel, n_tiles=n_tiles),
        grid=(bsz, B_HEADS),
        in_specs=[qk, qk, hv, hv],
        out_specs=hv,
        out_shape=jax.ShapeDtypeStruct((bsz, t, B_HEADS * B_VDIM), BF16),
        compiler_params=_cparams("arbitrary", "arbitrary"),
        name="mla_attn_prompt",
    )(qcat, kcat, v, z)


def _attn_sample_kernel(pt_ref, ql_ref, qpe_ref, cn_ref, pn_ref, ckv_hbm, kpe_hbm, o_ref,
                        cpage, ppage, sem_c, sem_p, cbuf, pbuf, m_sc, l_sc, acc_sc, *, n_pages, n_groups, n_split):
    b = pl.program_id(0)
    last_seq = pl.num_programs(0) - 1
    per = n_pages // n_split
    width = per * PAGE_SIZE

    def page_copies(seq, grp, slot):
        copies = []
        for g in range(n_pages):
            page = pt_ref[seq, grp * n_pages + g]
            copies.append(pltpu.make_async_copy(ckv_hbm.at[page], cpage.at[slot, g], sem_c.at[slot]))
            copies.append(pltpu.make_async_copy(kpe_hbm.at[page], ppage.at[slot, g], sem_p.at[slot]))
        return copies

    @pl.when(b == 0)
    def _():
        for copy in page_copies(0, 0, 0):
            copy.start()

    m_sc[...] = jnp.full(m_sc.shape, -jnp.inf, F32)
    l_sc[...] = jnp.zeros(l_sc.shape, F32)
    acc_sc[...] = jnp.zeros(acc_sc.shape, F32)
    pbuf[B_ROPE:, :] = jnp.zeros((pbuf.shape[0] - B_ROPE, pbuf.shape[1]), BF16)
    ql = ql_ref[...]
    qpe = qpe_ref[...]

    def group(k, carry):
        slot = lax.rem(b * n_groups + k, 2)
        last_grp = k == n_groups - 1

        @pl.when(jnp.logical_not(jnp.logical_and(last_grp, b == last_seq)))
        def _():
            nxt_seq = jnp.where(last_grp, b + 1, b)
            nxt_grp = jnp.where(last_grp, 0, k + 1)
            for copy in page_copies(nxt_seq, nxt_grp, 1 - slot):
                copy.start()

        for copy in page_copies(b, k, slot):
            copy.wait()

        scores = []
        for part in range(n_split):
            for g in range(part * per, (part + 1) * per):
                keys = pl.ds(g * PAGE_SIZE, PAGE_SIZE)
                cbuf[keys, :] = cpage[slot, g].astype(BF16)
                pbuf[0:B_ROPE, keys] = ppage[slot, g].astype(BF16)
            keys = pl.ds(part * width, width)
            scores.append(lax.dot_general(ql, cbuf[keys, :], NT_DIMS, preferred_element_type=F32)
                          + jnp.dot(qpe, pbuf[:, keys], preferred_element_type=F32))
        partials = []
        for part, s in enumerate(scores):
            m_part = jnp.max(s, axis=1, keepdims=True)
            p = jnp.exp(s - m_part)
            partials.append((m_part, jnp.sum(p, axis=1, keepdims=True),
                             jnp.dot(p.astype(BF16), cbuf[pl.ds(part * width, width), :],
                                     preferred_element_type=F32)))

        m_prev = m_sc[...]
        m_new = m_prev
        for m_part, _, _ in partials:
            m_new = jnp.maximum(m_new, m_part)
        w_prev = jnp.exp(m_prev - m_new)
        l_new = w_prev * l_sc[...]
        acc = w_prev * acc_sc[...]
        for m_part, l_part, a_part in partials:
            w_part = jnp.exp(m_part - m_new)
            l_new = l_new + w_part * l_part
            acc = acc + w_part * a_part
        m_sc[...] = m_new
        l_sc[...] = l_new
        acc_sc[...] = acc
        return carry

    lax.fori_loop(0, n_groups, group, 0)

    n_rows = ql.shape[0]
    s_len = cn_ref.shape[0]
    tok = lax.broadcasted_iota(jnp.int32, (n_rows, 1), 0) // B_HEADS
    qlf = ql.astype(F32)
    qpf = qpe.astype(F32)[:, :B_ROPE]
    m_run, l_run, acc_run = m_sc[...], l_sc[...], acc_sc[...]
    for t in range(s_len):
        c_row = cn_ref[t:t + 1, :]
        sc = (jnp.sum(qlf * c_row, axis=1, keepdims=True)
              + jnp.sum(qpf * pn_ref[t:t + 1, :], axis=1, keepdims=True))
        sc = jnp.where(tok >= t, sc, -jnp.inf)
        m_nxt = jnp.maximum(m_run, sc)
        a = jnp.exp(m_run - m_nxt)
        pt = jnp.exp(sc - m_nxt)
        l_run = a * l_run + pt
        acc_run = a * acc_run + pt * c_row
        m_run = m_nxt
    o_ref[...] = (acc_run / l_run).astype(o_ref.dtype)


def _attn_sample(page_table, ql, qpe, c_new, pe_new, cache_ckv, cache_kpe_t, *, n_pages, n_split):
    dbs, n_rows, _ = ql.shape
    s_len = c_new.shape[1]
    n_groups = page_table.shape[1] // n_pages
    seq = lambda *shape: pl.BlockSpec((None,) + shape, lambda b, pt: (b,) + (0,) * len(shape))
    hbm = pl.BlockSpec(memory_space=pl.ANY)
    grid_spec = pltpu.PrefetchScalarGridSpec(
        num_scalar_prefetch=1,
        grid=(dbs,),
        in_specs=[seq(n_rows, KV_LORA), seq(n_rows, LANES), seq(s_len, KV_LORA), seq(s_len, B_ROPE), hbm, hbm],
        out_specs=pl.BlockSpec((None, n_rows, KV_LORA), lambda b, pt: (b, 0, 0)),
        scratch_shapes=[pltpu.VMEM((2, n_pages, PAGE_SIZE, KV_LORA), cache_ckv.dtype),
                        pltpu.VMEM((2, n_pages, B_ROPE, PAGE_SIZE), cache_kpe_t.dtype),
                        pltpu.SemaphoreType.DMA((2,)), pltpu.SemaphoreType.DMA((2,)),
                        pltpu.VMEM((n_pages * PAGE_SIZE, KV_LORA), BF16),
                        pltpu.VMEM((LANES, n_pages * PAGE_SIZE), BF16),
                        pltpu.VMEM((n_rows, 1), F32), pltpu.VMEM((n_rows, 1), F32),
                        pltpu.VMEM((n_rows, KV_LORA), F32)],
    )
    return pl.pallas_call(
        functools.partial(_attn_sample_kernel, n_pages=n_pages, n_groups=n_groups, n_split=n_split),
        grid_spec=grid_spec,
        out_shape=jax.ShapeDtypeStruct((dbs, n_rows, KV_LORA), BF16),
        compiler_params=_cparams("arbitrary"),
        name="mla_attn_sample",
    )(page_table, ql, qpe, c_new, pe_new, cache_ckv, cache_kpe_t)


def _rope_tables(pos):
    half = B_ROPE // 2
    inv = ROPE_BASE ** (-jnp.arange(half, dtype=F32) / half)
    ang = pos[:, None] * inv[None, :]
    cos, sin = jnp.cos(ang), jnp.sin(ang)
    zero = jnp.zeros_like(cos)
    return (jnp.concatenate([cos, cos, zero, zero], axis=1),
            jnp.concatenate([-sin, sin, zero, zero], axis=1))


def _dup_rope_cols(w):
    half = B_ROPE // 2
    x1, x2 = w[..., :half], w[..., half:]
    return jnp.concatenate([x1, x2, x2, x1], axis=-1)


def kernel(x_prompt, x_sample, state_mlstm_C, state_mlstm_n, state_mlstm_m, cache_ckv, cache_kpe, page_table,
           meta_tokens, a_w_in, a_b_gates, a_norm_g, a_w_out, a_ln_g, a_ln_b, kv_w_down, kv_norm_g, kv_w_up,
           b_w_in, b_q_norm_g, b_w_qb, b_w_out, b_ln_g, b_ln_b):
    bsz, seq, d_model = x_prompt.shape
    dbs, s_len, _ = x_sample.shape
    n_a, n_b = a_w_in.shape[0], b_w_in.shape[0]
    alpha = (2 * (n_a + n_b)) ** 0.25
    dqk = d_model // (2 * A_HEADS)
    dv = d_model // A_HEADS
    qk_w, v_w = A_HEADS * dqk, A_HEADS * dv
    main_w = 2 * qk_w + 3 * v_w
    t = seq + N_META
    n_chunks = seq // A_CHUNK
    past = page_table.shape[1] * PAGE_SIZE
    mp, ms = bsz * t, dbs * s_len

    n_tiles = 3 if t % (3 * BF16_SUBLANES) == 0 else 1
    bm_p = t // n_tiles
    bm_ln = _pick_block(mp, 384)
    bm_big = _pick_block(mp, 1376)
    bm_s = _pick_block(ms, 512)
    n_pages = min(32, page_table.shape[1])
    n_split = n_pages // 4 if n_pages % 4 == 0 else 1
    assert s_len <= SAMPLE_CHUNK and page_table.shape[1] % n_pages == 0

    w_up3 = kv_w_up.reshape(KV_LORA, B_HEADS, B_NOPE + B_VDIM)
    w_uk, w_uv = w_up3[..., :B_NOPE], w_up3[..., B_NOPE:]
    ng = B_HEADS // HEAD_GROUP
    w_kv_perm = jnp.concatenate(
        [w_uk.reshape(KV_LORA, ng, HEAD_GROUP * B_NOPE), w_uv.reshape(KV_LORA, ng, HEAD_GROUP * B_VDIM)],
        axis=2).reshape(KV_LORA, -1).astype(BF16)
    w_ukt = jnp.transpose(w_uk, (1, 2, 0)).astype(BF16)
    w_uvh = jnp.transpose(w_uv, (1, 0, 2)).astype(BF16)
    w_dn_ext = jnp.concatenate([kv_w_down[:, :KV_LORA], _dup_rope_cols(kv_w_down[:, KV_LORA:])], axis=1).astype(BF16)

    pos_p = jnp.arange(t, dtype=F32)
    pos_s = past + jnp.arange(s_len, dtype=F32)
    cs_p, sn_p = _rope_tables(pos_p)
    cs_s, sn_s = (jnp.tile(a, (bm_s // s_len, 1)) for a in _rope_tables(pos_s))

    meta = jnp.broadcast_to(meta_tokens.astype(x_prompt.dtype)[None], (bsz, N_META, d_model))
    xp = jnp.concatenate([meta, x_prompt], axis=1).reshape(mp, d_model)
    xs = x_sample.reshape(ms, d_model)
    xp16 = xs16 = None
    cache_kpe_t = jnp.swapaxes(cache_kpe, 1, 2)

    st_p, st_s = [], []
    for i in range(n_a):
        w_main = a_w_in[i][:, :main_w].astype(BF16)
        w_g = jnp.pad(a_w_in[i][:, main_w:], ((0, 0), (0, LANES - 2 * A_HEADS))).astype(BF16)
        b_g = jnp.pad(a_b_gates[i], (0, LANES - 2 * A_HEADS)).reshape(1, LANES)
        w_out = a_w_out[i].astype(BF16)

        xp_in = xp if xp16 is None else xp16
        xs_in = xs if xs16 is None else xs16
        u = _mm_plain(xp_in, w_main, BF16, bm=bm_p, bn=2048, name="a_in_prompt")
        gates = _mm_gates(xp_in, w_g, b_g, bm=bm_p, name="a_gates_prompt")
        g4 = gates[:, :2 * A_HEADS].reshape(bsz, t, 2, A_HEADS).transpose(0, 3, 2, 1)
        g_meta = g4[..., :N_META]
        g_chunks = g4[..., N_META:].reshape(bsz, A_HEADS, 2, n_chunks, A_CHUNK).transpose(0, 1, 3, 2, 4)
        h_p, c_p, n_p, m_p = _mlstm_prompt(u.reshape(bsz, t, main_w), g_meta, g_chunks, dqk=dqk, dv=dv,
                                           heads=MLSTM_HEAD_GROUP)
        xp, xp16 = _mm_mlstm_out(h_p.reshape(mp, v_w), u, a_norm_g[i], w_out, xp, a_ln_g[i], a_ln_b[i], alpha,
                                 (F32, BF16), heads=A_HEADS, bm=bm_ln, name="a_out_prompt")
        st_p.append((c_p, n_p.reshape(bsz, A_HEADS, dqk), m_p[:, :, 0, 0]))

        us = _mm_plain(xs_in, w_main, BF16, bm=bm_s, bn=1024, name="a_in_sample")
        u_pad = jnp.pad(us[:, :2 * qk_w + v_w].reshape(dbs, s_len, 2 * qk_w + v_w),
                        ((0, 0), (0, SAMPLE_CHUNK - s_len), (0, 0)))
        gs = _mm_gates(xs_in, w_g, b_g, bm=bm_s, name="a_gates_sample")
        g3 = gs[:, :2 * A_HEADS].reshape(dbs, s_len, 2 * A_HEADS).transpose(0, 2, 1)
        tail = ((0, 0), (0, 0), (0, SAMPLE_CHUNK - s_len))
        g_rows = jnp.concatenate([jnp.pad(g3[:, :A_HEADS], tail, constant_values=-jnp.inf),
                                  jnp.pad(g3[:, A_HEADS:], tail)], axis=1)
        h_s, c_s, n_s, m_s = _mlstm_sample(u_pad, g_rows, state_mlstm_C[:, i], state_mlstm_n[:, i],
                                           state_mlstm_m[:, i][..., None], s_len=s_len, n_seq=1, dqk=dqk, dv=dv)
        xs, xs16 = _mm_mlstm_out(h_s.reshape(ms, v_w), us, a_norm_g[i], w_out, xs, a_ln_g[i], a_ln_b[i], alpha,
                                 (F32, BF16), heads=A_HEADS, bm=bm_s, name="a_out_sample")
        st_s.append((c_s, n_s, m_s[..., 0]))

    if xp16 is None:
        xp16, xs16 = xp.astype(BF16), xs.astype(BF16)
    for j in range(n_b):
        outs = (F32,) if j == n_b - 1 else (F32, BF16)
        if j == 0:
            ckv_p, kpe_p, ckv16_p, kpe16_p = _mm_kv_down(xp16, w_dn_ext, kv_norm_g, cs_p, sn_p, bm=bm_p,
                                                         name="kv_down_prompt")
            ckv_s, kpe_s, _, _ = _mm_kv_down(xs16, w_dn_ext, kv_norm_g, cs_s, sn_s, bm=bm_s, name="kv_down_sample")
            kcat, vcat = _mm_kv_up(ckv16_p, w_kv_perm, kpe16_p, bm=bm_p, name="kv_up_prompt")
        w_ql = b_w_in[j][:, :Q_LORA].astype(BF16)
        w_z = b_w_in[j][:, Q_LORA:].astype(BF16)
        w_qb3 = b_w_qb[j].reshape(Q_LORA, B_HEADS, B_NOPE + B_ROPE)
        w_q_perm = jnp.concatenate([w_qb3[..., :B_NOPE], _dup_rope_cols(w_qb3[..., B_NOPE:])],
                                   axis=2).reshape(Q_LORA, -1).astype(BF16)
        w_out = b_w_out[j].astype(BF16)

        qlat = _mm_rms(xp16, w_ql, b_q_norm_g[j], bm=bm_big, name="b_qlat_prompt")
        z = _mm_plain(xp16, w_z, BF16, bm=bm_big, bn=1024, name="b_z_prompt")
        qcat = _mm_q_up(qlat, w_q_perm, cs_p, sn_p, bm=bm_p, name="b_q_up_prompt")
        og = _attn_prompt(qcat.reshape(bsz, t, -1), kcat.reshape(bsz, t, -1), vcat.reshape(bsz, t, -1),
                          z.reshape(bsz, t, -1), n_tiles=n_tiles)
        if j == n_b - 1:
            y_prompt = _mm_resid_ln_tail(og, w_out, xp.reshape(bsz, t, d_model), b_ln_g[j], b_ln_b[j], alpha,
                                         skip=N_META, bm=_pick_block(seq, 512), name="b_out_prompt")
        else:
            xp, xp16 = _mm_resid_ln(og.reshape(mp, -1), w_out, xp, b_ln_g[j], b_ln_b[j], alpha, outs, bm=bm_ln,
                                    name="b_out_prompt")

        qlat_s = _mm_rms(xs16, w_ql, b_q_norm_g[j], bm=bm_s, name="b_qlat_sample")
        z_s = _mm_plain(xs16, w_z, BF16, bm=bm_s, bn=1024, name="b_z_sample")
        qcat_s = _mm_q_up(qlat_s, w_q_perm, cs_s, sn_s, bm=bm_s, name="b_q_up_sample")
        ql = _mm_q_absorb(qcat_s, w_ukt, name="b_q_absorb_sample")
        qpe = qcat_s.reshape(ms, B_HEADS, QK_HEAD_W)[:, :, B_NOPE:]
        olat = _attn_sample(page_table, ql.reshape(dbs, s_len * B_HEADS, KV_LORA),
                            qpe.reshape(dbs, s_len * B_HEADS, LANES),
                            ckv_s.reshape(dbs, s_len, KV_LORA), kpe_s.reshape(dbs, s_len, B_ROPE),
                            cache_ckv, cache_kpe_t, n_pages=n_pages, n_split=n_split)
        ogs = _mm_o_up(olat.reshape(ms, B_HEADS * KV_LORA), w_uvh, z_s, name="b_o_up_sample")
        xs, *rest = _mm_resid_ln(ogs, w_out, xs, b_ln_g[j], b_ln_b[j], alpha, outs, bm=bm_s, name="b_out_sample")
        xs16 = rest[0] if rest else None

    if n_b == 0:
        y_prompt = xp.reshape(bsz, t, d_model)[:, N_META:]
    stack = lambda parts, k: jnp.stack([p[k] for p in parts], axis=1)
    return (y_prompt, xs.reshape(dbs, s_len, d_model),
            stack(st_p, 0), stack(st_p, 1), stack(st_p, 2),
            ckv_p.reshape(bsz, t, KV_LORA), kpe_p.reshape(bsz, t, B_ROPE),
            stack(st_s, 0), stack(st_s, 1), stack(st_s, 2),
            ckv_s.reshape(dbs, s_len, KV_LORA), kpe_s.reshape(dbs, s_len, B_ROPE))
```

```python
import functools

import jax
import jax.numpy as jnp
from jax import lax
from jax.experimental import pallas as pl
from jax.experimental.pallas import tpu as pltpu

F32 = jnp.float32
BF16 = jnp.bfloat16

N_META = 16
A_HEADS = 8
A_CHUNK = 64
IGATE_CAP = 15.0
B_HEADS = 16
B_NOPE = 128
B_ROPE = 64
B_VDIM = 128
KV_LORA = 512
Q_LORA = 512
PAGE_SIZE = 128
ROPE_BASE = 10000.0
LN_EPS = 1e-5
RMS_EPS = 1e-6
B_SCALE = (B_NOPE + B_ROPE) ** -0.5

LANES = 128
BF16_SUBLANES = 16
VMEM_LIMIT_BYTES = 56 * 1024 * 1024
HEAD_GROUP = 4
MLSTM_HEAD_GROUP = 4
SAMPLE_CHUNK = BF16_SUBLANES
QK_HEAD_W = 2 * LANES

NT_DIMS = (((1,), (1,)), ((), ()))
TN_DIMS = (((0,), (0,)), ((), ()))


def _cparams(*sem):
    return pltpu.CompilerParams(dimension_semantics=sem, vmem_limit_bytes=VMEM_LIMIT_BYTES)


def _pick_block(m, target, mult=BF16_SUBLANES):
    best = None
    for d in range(mult, min(m, target) + 1, mult):
        if m % d == 0:
            best = d
    return best if best is not None else m


def _layer_norm(y, g, b):
    mu = jnp.mean(y, axis=-1, keepdims=True)
    yc = y - mu
    var = jnp.mean(yc * yc, axis=-1, keepdims=True)
    return yc * lax.rsqrt(var + LN_EPS) * g + b


def _rms_norm(y, g):
    return y * lax.rsqrt(jnp.mean(y * y, axis=-1, keepdims=True) + RMS_EPS) * g


def _rope_dup(v, cs, sn):
    return v * cs + pltpu.roll(v, 2 * (B_ROPE // 2), axis=1) * sn


def _silu(z):
    return z * jax.nn.sigmoid(z)


def _mm_plain_kernel(x_ref, w_ref, o_ref):
    acc = jnp.dot(x_ref[...].astype(BF16), w_ref[...], preferred_element_type=F32)
    o_ref[...] = acc.astype(o_ref.dtype)


def _mm_plain(x, w, out_dtype, *, bm, bn, name):
    m, k = x.shape
    n = w.shape[1]
    return pl.pallas_call(
        _mm_plain_kernel,
        grid=(n // bn, m // bm),
        in_specs=[pl.BlockSpec((bm, k), lambda j, i: (i, 0)),
                  pl.BlockSpec((k, bn), lambda j, i: (0, j))],
        out_specs=pl.BlockSpec((bm, bn), lambda j, i: (i, j)),
        out_shape=jax.ShapeDtypeStruct((m, n), out_dtype),
        compiler_params=_cparams("arbitrary", "arbitrary"),
        name=name,
    )(x, w)


def _mm_out_gate_kernel(x_ref, wo_ref, wz_ref, g_ref):
    x = x_ref[...].astype(BF16)
    o = jnp.dot(x, wo_ref[...], preferred_element_type=F32)
    z = jnp.dot(x, wz_ref[...], preferred_element_type=F32)
    g_ref[...] = (jax.nn.sigmoid(o) * _silu(z)).astype(g_ref.dtype)


def _mm_out_gate(x, w_o, w_z, *, bm, bn, name):
    m, k = x.shape
    n = w_o.shape[1]
    wspec = pl.BlockSpec((k, bn), lambda j, i: (0, j))
    return pl.pallas_call(
        _mm_out_gate_kernel,
        grid=(n // bn, m // bm),
        in_specs=[pl.BlockSpec((bm, k), lambda j, i: (i, 0)), wspec, wspec],
        out_specs=pl.BlockSpec((bm, bn), lambda j, i: (i, j)),
        out_shape=jax.ShapeDtypeStruct((m, n), BF16),
        compiler_params=_cparams("arbitrary", "arbitrary"),
        name=name,
    )(x, w_o, w_z)


def _mm_gates_kernel(x_ref, w_ref, b_ref, o_ref):
    a = jnp.dot(x_ref[...].astype(BF16), w_ref[...], preferred_element_type=F32) + b_ref[...]
    lane = lax.broadcasted_iota(jnp.int32, a.shape, 1)
    gi = IGATE_CAP * jnp.tanh(a / IGATE_CAP)
    lf = jnp.minimum(a, 0.0) - jnp.log1p(jnp.exp(-jnp.abs(a)))
    o_ref[...] = jnp.where(lane < A_HEADS, gi, lf)


def _mm_gates(x, w, b, *, bm, name):
    m, k = x.shape
    return pl.pallas_call(
        _mm_gates_kernel,
        grid=(m // bm,),
        in_specs=[pl.BlockSpec((bm, k), lambda i: (i, 0)),
                  pl.BlockSpec((k, LANES), lambda i: (0, 0)),
                  pl.BlockSpec((1, LANES), lambda i: (0, 0))],
        out_specs=pl.BlockSpec((bm, LANES), lambda i: (i, 0)),
        out_shape=jax.ShapeDtypeStruct((m, LANES), F32),
        compiler_params=_cparams("arbitrary"),
        name=name,
    )(x, w, b)


def _mm_resid_ln_kernel(x_ref, w_ref, r_ref, g_ref, b_ref, *o_refs, alpha):
    acc = jnp.dot(x_ref[...].astype(BF16), w_ref[...], preferred_element_type=F32)
    y = _layer_norm(alpha * r_ref[...] + acc, g_ref[...], b_ref[...])
    for o_ref in o_refs:
        o_ref[...] = y.astype(o_ref.dtype)


def _mm_resid_ln(x, w, resid, g, b, alpha, out_dtypes, *, bm, name):
    m, k = x.shape
    n = w.shape[1]
    row = lambda i: (i, 0)
    fixed = lambda i: (0, 0)
    outs = pl.pallas_call(
        functools.partial(_mm_resid_ln_kernel, alpha=alpha),
        grid=(m // bm,),
        in_specs=[pl.BlockSpec((bm, k), row), pl.BlockSpec((k, n), fixed),
                  pl.BlockSpec((bm, n), row), pl.BlockSpec((1, n), fixed), pl.BlockSpec((1, n), fixed)],
        out_specs=[pl.BlockSpec((bm, n), row) for _ in out_dtypes],
        out_shape=[jax.ShapeDtypeStruct((m, n), dt) for dt in out_dtypes],
        compiler_params=_cparams("arbitrary"),
        name=name,
    )(x, w, resid, g.reshape(1, n), b.reshape(1, n))
    return outs


def _mm_mlstm_out_kernel(h_ref, gate_ref, ng_ref, w_ref, r_ref, g_ref, b_ref, *out_refs, alpha, heads):
    dv = h_ref.shape[1] // heads
    parts = []
    for hd in range(heads):
        cols = slice(hd * dv, (hd + 1) * dv)
        h = h_ref[:, cols].astype(F32)
        hn = h * lax.rsqrt(jnp.mean(h * h, axis=1, keepdims=True) + RMS_EPS) * ng_ref[:, cols]
        parts.append((hn * gate_ref[:, cols].astype(F32)).astype(BF16))
    acc = jnp.dot(jnp.concatenate(parts, axis=1), w_ref[...], preferred_element_type=F32)
    y = _layer_norm(alpha * r_ref[...] + acc, g_ref[...], b_ref[...])
    for out_ref in out_refs:
        out_ref[...] = y.astype(out_ref.dtype)


def _mm_mlstm_out(h, gate, norm_g, w, resid, g, b, alpha, out_dtypes, *, heads, bm, name):
    m, k = h.shape
    n = w.shape[1]
    row = lambda i: (i, 0)
    fixed = lambda i: (0, 0)
    return pl.pallas_call(
        functools.partial(_mm_mlstm_out_kernel, alpha=alpha, heads=heads),
        grid=(m // bm,),
        in_specs=[pl.BlockSpec((bm, k), row), pl.BlockSpec((bm, k), row), pl.BlockSpec((1, k), fixed),
                  pl.BlockSpec((k, n), fixed), pl.BlockSpec((bm, n), row),
                  pl.BlockSpec((1, n), fixed), pl.BlockSpec((1, n), fixed)],
        out_specs=[pl.BlockSpec((bm, n), row) for _ in out_dtypes],
        out_shape=[jax.ShapeDtypeStruct((m, n), dt) for dt in out_dtypes],
        compiler_params=_cparams("arbitrary"),
        name=name,
    )(h, gate, norm_g.reshape(1, k), w, resid, g.reshape(1, n), b.reshape(1, n))


def _mm_resid_ln_tail_kernel(x_ref, w_ref, r_ref, g_ref, b_ref, o_ref, *, alpha):
    acc = jnp.dot(x_ref[0].astype(BF16), w_ref[...], preferred_element_type=F32)
    o_ref[...] = _layer_norm(alpha * r_ref[0] + acc, g_ref[...], b_ref[...])


def _mm_resid_ln_tail(x, w, resid, g, b, alpha, *, skip, bm, name):
    bsz, t, k = x.shape
    n = w.shape[1]
    rows = t - skip
    tail = lambda width: pl.BlockSpec((pl.Element(1), pl.Element(bm), pl.Element(width)),
                                      lambda bi, i: (bi, pl.multiple_of(skip + i * bm, BF16_SUBLANES), 0))
    fixed = lambda bi, i: (0, 0)
    return pl.pallas_call(
        functools.partial(_mm_resid_ln_tail_kernel, alpha=alpha),
        grid=(bsz, rows // bm),
        in_specs=[tail(k), pl.BlockSpec((k, n), fixed), tail(n),
                  pl.BlockSpec((1, n), fixed), pl.BlockSpec((1, n), fixed)],
        out_specs=[pl.BlockSpec((None, bm, n), lambda bi, i: (bi, i, 0))],
        out_shape=[jax.ShapeDtypeStruct((bsz, rows, n), F32)],
        compiler_params=_cparams("arbitrary", "arbitrary"),
        name=name,
    )(x, w, resid, g.reshape(1, n), b.reshape(1, n))[0]


def _mm_kv_down_kernel(x_ref, w_ref, g_ref, cs_ref, sn_ref, ckv_ref, kpe_ref, ckv16_ref, kpe16_ref):
    acc = jnp.dot(x_ref[...].astype(BF16), w_ref[...], preferred_element_type=F32)
    ckv = _rms_norm(acc[:, :KV_LORA], g_ref[...])
    pe = _rope_dup(acc[:, KV_LORA:], cs_ref[...], sn_ref[...])
    ckv_ref[...] = ckv
    ckv16_ref[...] = ckv.astype(BF16)
    kpe_ref[...] = pe[:, :B_ROPE]
    kpe16_ref[...] = pe.astype(BF16)


def _mm_kv_down(x, w_ext, g, cs, sn, *, bm, name):
    m, k = x.shape
    n = w_ext.shape[1]
    nt = cs.shape[0] // bm
    row = lambda i: (i, 0)
    fixed = lambda i: (0, 0)
    tab = lambda i: (i % nt, 0)
    return pl.pallas_call(
        _mm_kv_down_kernel,
        grid=(m // bm,),
        in_specs=[pl.BlockSpec((bm, k), row), pl.BlockSpec((k, n), fixed), pl.BlockSpec((1, KV_LORA), fixed),
                  pl.BlockSpec((bm, LANES), tab), pl.BlockSpec((bm, LANES), tab)],
        out_specs=[pl.BlockSpec((bm, KV_LORA), row), pl.BlockSpec((bm, B_ROPE), row),
                   pl.BlockSpec((bm, KV_LORA), row), pl.BlockSpec((bm, LANES), row)],
        out_shape=[jax.ShapeDtypeStruct((m, KV_LORA), F32), jax.ShapeDtypeStruct((m, B_ROPE), F32),
                   jax.ShapeDtypeStruct((m, KV_LORA), BF16), jax.ShapeDtypeStruct((m, LANES), BF16)],
        compiler_params=_cparams("arbitrary"),
        name=name,
    )(x, w_ext, g.reshape(1, KV_LORA), cs, sn)


def _mm_rms_kernel(x_ref, w_ref, g_ref, o_ref):
    acc = jnp.dot(x_ref[...].astype(BF16), w_ref[...], preferred_element_type=F32)
    o_ref[...] = _rms_norm(acc, g_ref[...]).astype(o_ref.dtype)


def _mm_rms(x, w, g, *, bm, name):
    m, k = x.shape
    n = w.shape[1]
    return pl.pallas_call(
        _mm_rms_kernel,
        grid=(m // bm,),
        in_specs=[pl.BlockSpec((bm, k), lambda i: (i, 0)), pl.BlockSpec((k, n), lambda i: (0, 0)),
                  pl.BlockSpec((1, n), lambda i: (0, 0))],
        out_specs=pl.BlockSpec((bm, n), lambda i: (i, 0)),
        out_shape=jax.ShapeDtypeStruct((m, n), BF16),
        compiler_params=_cparams("arbitrary"),
        name=name,
    )(x, w, g.reshape(1, n))


def _mm_q_up_kernel(x_ref, w_ref, cs_ref, sn_ref, q_ref):
    acc = jnp.dot(x_ref[...], w_ref[...], preferred_element_type=F32) * B_SCALE
    cs = cs_ref[...]
    sn = sn_ref[...]
    for h in range(HEAD_GROUP):
        lo = h * QK_HEAD_W
        q_ref[:, lo:lo + B_NOPE] = acc[:, lo:lo + B_NOPE].astype(BF16)
        q_ref[:, lo + B_NOPE:lo + QK_HEAD_W] = _rope_dup(acc[:, lo + B_NOPE:lo + QK_HEAD_W], cs, sn).astype(BF16)


def _mm_q_up(qlat, w_perm, cs, sn, *, bm, name):
    m, k = qlat.shape
    ng = B_HEADS // HEAD_GROUP
    wn = HEAD_GROUP * QK_HEAD_W
    nt = cs.shape[0] // bm
    return pl.pallas_call(
        _mm_q_up_kernel,
        grid=(ng, m // bm),
        in_specs=[pl.BlockSpec((bm, k), lambda j, i: (i, 0)),
                  pl.BlockSpec((k, wn), lambda j, i: (0, j)),
                  pl.BlockSpec((bm, LANES), lambda j, i: (i % nt, 0)),
                  pl.BlockSpec((bm, LANES), lambda j, i: (i % nt, 0))],
        out_specs=pl.BlockSpec((bm, wn), lambda j, i: (i, j)),
        out_shape=jax.ShapeDtypeStruct((m, B_HEADS * QK_HEAD_W), BF16),
        compiler_params=_cparams("arbitrary", "arbitrary"),
        name=name,
    )(qlat, w_perm, cs, sn)


def _mm_heads_kernel(x_ref, w_ref, o_ref):
    o_ref[...] = jnp.dot(x_ref[...].astype(BF16), w_ref[...], preferred_element_type=F32).astype(o_ref.dtype)


def _mm_q_absorb(qcat, w_ukt, *, name):
    m = qcat.shape[0]
    return pl.pallas_call(
        _mm_heads_kernel,
        grid=(B_HEADS,),
        in_specs=[pl.BlockSpec((m, B_NOPE), lambda h: (0, 2 * h)),
                  pl.BlockSpec((None, B_NOPE, KV_LORA), lambda h: (h, 0, 0))],
        out_specs=pl.BlockSpec((m, KV_LORA), lambda h: (0, h)),
        out_shape=jax.ShapeDtypeStruct((m, B_HEADS * KV_LORA), BF16),
        compiler_params=_cparams("arbitrary"),
        name=name,
    )(qcat, w_ukt)


def _mm_o_up_kernel(x_ref, w_ref, z_ref, o_ref):
    acc = jnp.dot(x_ref[...].astype(BF16), w_ref[...], preferred_element_type=F32)
    o_ref[...] = (acc * _silu(z_ref[...].astype(F32))).astype(o_ref.dtype)


def _mm_o_up(olat, w_uv, z, *, name):
    m = olat.shape[0]
    return pl.pallas_call(
        _mm_o_up_kernel,
        grid=(B_HEADS,),
        in_specs=[pl.BlockSpec((m, KV_LORA), lambda h: (0, h)),
                  pl.BlockSpec((None, KV_LORA, B_VDIM), lambda h: (h, 0, 0)),
                  pl.BlockSpec((m, B_VDIM), lambda h: (0, h))],
        out_specs=pl.BlockSpec((m, B_VDIM), lambda h: (0, h)),
        out_shape=jax.ShapeDtypeStruct((m, B_HEADS * B_VDIM), BF16),
        compiler_params=_cparams("arbitrary"),
        name=name,
    )(olat, w_uv, z)


def _mlstm_gate_terms(lf_row, gi_row, m_prev):
    length = lf_row.shape[1]
    t_idx = lax.broadcasted_iota(jnp.int32, (length, length), 0)
    s_idx = lax.broadcasted_iota(jnp.int32, (length, length), 1)
    causal = s_idx <= t_idx
    eye = s_idx == t_idx
    b_col = jnp.sum(jnp.where(causal, lf_row, 0.0), axis=1, keepdims=True)
    b_row = jnp.sum(jnp.where(eye, b_col, 0.0), axis=0, keepdims=True)
    gi_col = jnp.sum(jnp.where(eye, gi_row, 0.0), axis=1, keepdims=True)
    log_d = jnp.where(causal, b_col - b_row + gi_row, -jnp.inf)
    m_t = jnp.maximum(jnp.max(log_d, axis=1, keepdims=True), b_col + m_prev)
    w_inter = jnp.exp(b_col + m_prev - m_t)
    d = jnp.exp(log_d - m_t)
    b_end = b_row[:, length - 1:length]
    log_w_row = b_end - b_row + gi_row
    m_new = jnp.maximum(b_end + m_prev, jnp.max(log_w_row, axis=1, keepdims=True))
    w_c = jnp.exp(b_end + m_prev - m_new)
    w_s_col = jnp.exp(b_end - b_col + gi_col - m_new)
    return d, m_t, w_inter, w_s_col, w_c, m_new


def _mlstm_chunk_heads(heads, scale):
    stage_a = []
    for q, k, v, gi_row, lf_row, c_prev, n_prev, m_prev in heads:
        terms = _mlstm_gate_terms(lf_row, gi_row, m_prev)
        qk = lax.dot_general(q, k, NT_DIMS, preferred_element_type=F32)
        q_c = jnp.dot(q, c_prev.astype(BF16), preferred_element_type=F32)
        stage_a.append((terms, qk, q_c))
    stage_b = []
    for (q, k, v, _, _, c_prev, n_prev, _), (terms, qk, q_c) in zip(heads, stage_a):
        d, m_t, w_inter, w_s_col, w_c, m_new = terms
        s = qk * (scale * d)
        s_v = jnp.dot(s.astype(BF16), v, preferred_element_type=F32)
        w_v = (w_s_col * v.astype(F32)).astype(BF16)
        k_v = lax.dot_general(k, w_v, TN_DIMS, preferred_element_type=F32)
        stage_b.append((s, s_v, k_v))
    results = []
    for (q, k, v, _, _, c_prev, n_prev, _), (terms, _, q_c), (s, s_v, k_v) in zip(heads, stage_a, stage_b):
        d, m_t, w_inter, w_s_col, w_c, m_new = terms
        w_q = w_inter * scale
        num = w_q * q_c + s_v
        q_n = jnp.sum(q.astype(F32) * n_prev, axis=1, keepdims=True)
        den = w_q * q_n + jnp.sum(s, axis=1, keepdims=True)
        h = num / jnp.maximum(jnp.abs(den), jnp.exp(-m_t))
        c_new = w_c * c_prev + k_v
        n_new = w_c * n_prev + jnp.sum(w_s_col * k.astype(F32), axis=0, keepdims=True)
        results.append((h, c_new, n_new, m_new))
    return results


def _mlstm_prompt_kernel(q_ref, k_ref, v_ref, gm_ref, gc_ref, h_ref, c_ref, n_ref, m_out_ref, *, n_chunks, heads):
    dqk = q_ref.shape[1] // heads
    dv = v_ref.shape[1] // heads
    scale = dqk ** -0.5

    def run_chunk(rows, gates, m_prev):
        inputs = []
        for h in range(heads):
            qk_cols = slice(h * dqk, (h + 1) * dqk)
            v_cols = slice(h * dv, (h + 1) * dv)
            inputs.append((q_ref[rows, qk_cols], k_ref[rows, qk_cols], v_ref[rows, v_cols], *gates(h),
                           c_ref[h], n_ref[h], m_prev[h]))
        m_next = []
        for h, (out, c_new, n_new, m_new) in enumerate(_mlstm_chunk_heads(inputs, scale)):
            h_ref[rows, h * dv:(h + 1) * dv] = out.astype(h_ref.dtype)
            c_ref[h] = c_new
            n_ref[h] = n_new
            m_next.append(m_new)
        return tuple(m_next)

    c_ref[...] = jnp.zeros(c_ref.shape, F32)
    n_ref[...] = jnp.zeros(n_ref.shape, F32)
    m_state = tuple(jnp.zeros((1, 1), F32) for _ in range(heads))
    m_state = run_chunk(pl.ds(0, N_META), lambda h: (gm_ref[h, 0:1, :], gm_ref[h, 1:2, :]), m_state)

    def body(c, m_prev):
        r0 = pl.multiple_of(N_META + c * A_CHUNK, BF16_SUBLANES)
        return run_chunk(pl.ds(r0, A_CHUNK), lambda h: (gc_ref[h, c, 0:1, :], gc_ref[h, c, 1:2, :]), m_prev)

    m_state = lax.fori_loop(0, n_chunks, body, m_state)
    for h in range(heads):
        m_out_ref[h] = jnp.broadcast_to(m_state[h], m_out_ref.shape[1:])


def _mlstm_prompt(u, g_meta, g_chunks, *, dqk, dv, heads):
    bsz, t, _ = u.shape
    n_chunks = g_chunks.shape[2]
    groups = A_HEADS // heads
    v_base = 2 * A_HEADS * dqk // (heads * dv)
    tok = lambda width, base: pl.BlockSpec((None, t, heads * width), lambda b, g: (b, 0, base + g))
    hsp = lambda *shape: pl.BlockSpec((None, heads) + shape, lambda b, g: (b, g) + (0,) * len(shape))
    return pl.pallas_call(
        functools.partial(_mlstm_prompt_kernel, n_chunks=n_chunks, heads=heads),
        grid=(bsz, groups),
        in_specs=[tok(dqk, 0), tok(dqk, groups), tok(dv, v_base), hsp(2, N_META), hsp(n_chunks, 2, A_CHUNK)],
        out_specs=[pl.BlockSpec((None, t, heads * dv), lambda b, g: (b, 0, g)),
                   hsp(dqk, dv), hsp(1, dqk), hsp(1, LANES)],
        out_shape=[jax.ShapeDtypeStruct((bsz, t, A_HEADS * dv), BF16),
                   jax.ShapeDtypeStruct((bsz, A_HEADS, dqk, dv), F32),
                   jax.ShapeDtypeStruct((bsz, A_HEADS, 1, dqk), F32),
                   jax.ShapeDtypeStruct((bsz, A_HEADS, 1, LANES), F32)],
        compiler_params=_cparams("arbitrary", "arbitrary"),
        name="mlstm_prompt",
    )(u, u, u, g_meta, g_chunks)


def _mlstm_sample_kernel(u_ref, g_ref, c_ref, n_ref, m_ref, h_ref, c_out_ref, n_out_ref, m_out_ref, *, dqk, dv):
    n_seq, s_len = h_ref.shape[0], h_ref.shape[1]
    scale = dqk ** -0.5
    k_base = A_HEADS * dqk
    v_base = 2 * A_HEADS * dqk
    inputs = []
    for i in range(n_seq):
        for h in range(A_HEADS):
            inputs.append((
                u_ref[i, :, h * dqk:(h + 1) * dqk], u_ref[i, :, k_base + h * dqk:k_base + (h + 1) * dqk],
                u_ref[i, :, v_base + h * dv:v_base + (h + 1) * dv],
                g_ref[i, h:h + 1, :], g_ref[i, A_HEADS + h:A_HEADS + h + 1, :],
                c_ref[i, h], n_ref[i, h:h + 1, :], m_ref[i, h:h + 1, :]))
    for idx, (out, c_new, n_new, m_new) in enumerate(_mlstm_chunk_heads(inputs, scale)):
        i, h = divmod(idx, A_HEADS)
        h_ref[i, :, h * dv:(h + 1) * dv] = out[:s_len]
        c_out_ref[i, h] = c_new
        n_out_ref[i, h:h + 1, :] = n_new
        m_out_ref[i, h:h + 1, :] = m_new


def _mlstm_sample(u_pad, g_rows, c_prev, n_prev, m_prev, *, s_len, n_seq, dqk, dv):
    dbs, length, width = u_pad.shape
    seq3 = lambda *shape: pl.BlockSpec((n_seq,) + shape, lambda b: (b,) + (0,) * len(shape))
    return pl.pallas_call(
        functools.partial(_mlstm_sample_kernel, dqk=dqk, dv=dv),
        grid=(dbs // n_seq,),
        in_specs=[seq3(length, width), seq3(2 * A_HEADS, length),
                  seq3(A_HEADS, dqk, dv), seq3(A_HEADS, dqk), seq3(A_HEADS, 1)],
        out_specs=[seq3(s_len, A_HEADS * dv), seq3(A_HEADS, dqk, dv), seq3(A_HEADS, dqk), seq3(A_HEADS, 1)],
        out_shape=[jax.ShapeDtypeStruct((dbs, s_len, A_HEADS * dv), F32),
                   jax.ShapeDtypeStruct((dbs, A_HEADS, dqk, dv), F32),
                   jax.ShapeDtypeStruct((dbs, A_HEADS, dqk), F32),
                   jax.ShapeDtypeStruct((dbs, A_HEADS, 1), F32)],
        compiler_params=_cparams("arbitrary"),
        name="mlstm_sample",
    )(u_pad, g_rows, c_prev, n_prev, m_prev)


def _attn_prompt_kernel(qlat_ref, wq_ref, ckv_ref, wkv_ref, kpe_ref, cs_ref, sn_ref, z_ref, o_ref,
                        q_buf, k_buf, v_buf, *, n_tiles):
    kv = jnp.dot(ckv_ref[...], wkv_ref[...], preferred_element_type=F32)
    k_buf[:, :B_NOPE] = kv[:, :B_NOPE].astype(BF16)
    k_buf[:, B_NOPE:] = kpe_ref[...]
    v_buf[...] = kv[:, B_NOPE:].astype(BF16)
    qf = jnp.dot(qlat_ref[...], wq_ref[...], preferred_element_type=F32) * B_SCALE
    q_buf[:, :B_NOPE] = qf[:, :B_NOPE].astype(BF16)
    q_buf[:, B_NOPE:] = _rope_dup(qf[:, B_NOPE:], cs_ref[...], sn_ref[...]).astype(BF16)

    tq = q_buf.shape[0] // n_tiles
    row = lax.broadcasted_iota(jnp.int32, (tq, tq), 0)
    col = lax.broadcasted_iota(jnp.int32, (tq, tq), 1)
    for i in range(n_tiles):
        rows = pl.ds(i * tq, tq)
        q = q_buf[rows, :]
        m = jnp.full((tq, 1), -jnp.inf, F32)
        l = jnp.zeros((tq, 1), F32)
        acc = jnp.zeros((tq, v_buf.shape[1]), F32)
        for j in range(i + 1):
            keys = pl.ds(j * tq, tq)
            s = lax.dot_general(q, k_buf[keys, :], NT_DIMS, preferred_element_type=F32)
            if j == i:
                s = jnp.where(col <= row, s, -jnp.inf)
            m_new = jnp.maximum(m, jnp.max(s, axis=1, keepdims=True))
            alpha = jnp.exp(m - m_new)
            p = jnp.exp(s - m_new)
            l = alpha * l + jnp.sum(p, axis=1, keepdims=True)
            acc = alpha * acc + jnp.dot(p.astype(BF16), v_buf[keys, :], preferred_element_type=F32)
            m = m_new
        o_ref[rows, :] = ((acc / l) * _silu(z_ref[rows, :].astype(F32))).astype(o_ref.dtype)


def _attn_prompt(qlat, w_q, ckv16, w_kv, kpe16, cs, sn, z, *, n_tiles):
    bsz, t, _ = qlat.shape
    per_b = lambda width: pl.BlockSpec((None, t, width), lambda b, h: (b, 0, 0))
    per_h = lambda rows, width: pl.BlockSpec((rows, width), lambda b, h: (0, h))
    tab = pl.BlockSpec((t, LANES), lambda b, h: (0, 0))
    hv = pl.BlockSpec((None, t, B_VDIM), lambda b, h: (b, 0, h))
    return pl.pallas_call(
        functools.partial(_attn_prompt_kernel, n_tiles=n_tiles),
        grid=(bsz, B_HEADS),
        in_specs=[per_b(Q_LORA), per_h(Q_LORA, QK_HEAD_W), per_b(KV_LORA), per_h(KV_LORA, B_NOPE + B_VDIM),
                  per_b(LANES), tab, tab, hv],
        out_specs=hv,
        out_shape=jax.ShapeDtypeStruct((bsz, t, B_HEADS * B_VDIM), BF16),
        scratch_shapes=[pltpu.VMEM((t, QK_HEAD_W), BF16), pltpu.VMEM((t, QK_HEAD_W), BF16),
                        pltpu.VMEM((t, B_VDIM), BF16)],
        compiler_params=_cparams("arbitrary", "arbitrary"),
        name="mla_attn_prompt",
    )(qlat, w_q, ckv16, w_kv, kpe16, cs, sn, z)


def _attn_sample_kernel(pt_ref, ql_ref, qpe_ref, cn_ref, pn_ref, ckv_hbm, kpe_hbm, o_ref,
                        cpage, ppage, sem_c, sem_p, cbuf, pbuf, m_sc, l_sc, acc_sc, *, n_pages, n_groups, n_split):
    b = pl.program_id(0)
    last_seq = pl.num_programs(0) - 1
    per = n_pages // n_split
    width = per * PAGE_SIZE

    def page_copies(seq, grp, slot):
        copies = []
        for g in range(n_pages):
            page = pt_ref[seq, grp * n_pages + g]
            copies.append(pltpu.make_async_copy(ckv_hbm.at[page], cpage.at[slot, g], sem_c.at[slot]))
            copies.append(pltpu.make_async_copy(kpe_hbm.at[page], ppage.at[slot, g], sem_p.at[slot]))
        return copies

    @pl.when(b == 0)
    def _():
        for copy in page_copies(0, 0, 0):
            copy.start()

    m_sc[...] = jnp.full(m_sc.shape, -jnp.inf, F32)
    l_sc[...] = jnp.zeros(l_sc.shape, F32)
    acc_sc[...] = jnp.zeros(acc_sc.shape, F32)
    pbuf[B_ROPE:, :] = jnp.zeros((pbuf.shape[0] - B_ROPE, pbuf.shape[1]), BF16)
    ql = ql_ref[...]
    qpe = qpe_ref[...]

    def group(k, carry):
        slot = lax.rem(b * n_groups + k, 2)
        last_grp = k == n_groups - 1

        @pl.when(jnp.logical_not(jnp.logical_and(last_grp, b == last_seq)))
        def _():
            nxt_seq = jnp.where(last_grp, b + 1, b)
            nxt_grp = jnp.where(last_grp, 0, k + 1)
            for copy in page_copies(nxt_seq, nxt_grp, 1 - slot):
                copy.start()

        for copy in page_copies(b, k, slot):
            copy.wait()

        scores = []
        for part in range(n_split):
            for g in range(part * per, (part + 1) * per):
                keys = pl.ds(g * PAGE_SIZE, PAGE_SIZE)
                cbuf[keys, :] = cpage[slot, g].astype(BF16)
                pbuf[0:B_ROPE, keys] = ppage[slot, g].astype(BF16)
            keys = pl.ds(part * width, width)
            scores.append(lax.dot_general(ql, cbuf[keys, :], NT_DIMS, preferred_element_type=F32)
                          + jnp.dot(qpe, pbuf[:, keys], preferred_element_type=F32))
        partials = []
        for part, s in enumerate(scores):
            m_part = jnp.max(s, axis=1, keepdims=True)
            p = jnp.exp(s - m_part)
            partials.append((m_part, jnp.sum(p, axis=1, keepdims=True),
                             jnp.dot(p.astype(BF16), cbuf[pl.ds(part * width, width), :],
                                     preferred_element_type=F32)))

        m_prev = m_sc[...]
        m_new = m_prev
        for m_part, _, _ in partials:
            m_new = jnp.maximum(m_new, m_part)
        w_prev = jnp.exp(m_prev - m_new)
        l_new = w_prev * l_sc[...]
        acc = w_prev * acc_sc[...]
        for m_part, l_part, a_part in partials:
            w_part = jnp.exp(m_part - m_new)
            l_new = l_new + w_part * l_part
            acc = acc + w_part * a_part
        m_sc[...] = m_new
        l_sc[...] = l_new
        acc_sc[...] = acc
        return carry

    lax.fori_loop(0, n_groups, group, 0)

    n_rows = ql.shape[0]
    s_len = cn_ref.shape[0]
    tok = lax.broadcasted_iota(jnp.int32, (n_rows, 1), 0) // B_HEADS
    qlf = ql.astype(F32)
    qpf = qpe.astype(F32)[:, :B_ROPE]
    m_run, l_run, acc_run = m_sc[...], l_sc[...], acc_sc[...]
    for t in range(s_len):
        c_row = cn_ref[t:t + 1, :]
        sc = (jnp.sum(qlf * c_row, axis=1, keepdims=True)
              + jnp.sum(qpf * pn_ref[t:t + 1, :], axis=1, keepdims=True))
        sc = jnp.where(tok >= t, sc, -jnp.inf)
        m_nxt = jnp.maximum(m_run, sc)
        a = jnp.exp(m_run - m_nxt)
        pt = jnp.exp(sc - m_nxt)
        l_run = a * l_run + pt
        acc_run = a * acc_run + pt * c_row
        m_run = m_nxt
    o_ref[...] = (acc_run / l_run).astype(o_ref.dtype)


def _attn_sample(page_table, ql, qpe, c_new, pe_new, cache_ckv, cache_kpe_t, *, n_pages, n_split):
    dbs, n_rows, _ = ql.shape
    s_len = c_new.shape[1]
    n_groups = page_table.shape[1] // n_pages
    seq = lambda *shape: pl.BlockSpec((None,) + shape, lambda b, pt: (b,) + (0,) * len(shape))
    hbm = pl.BlockSpec(memory_space=pl.ANY)
    grid_spec = pltpu.PrefetchScalarGridSpec(
        num_scalar_prefetch=1,
        grid=(dbs,),
        in_specs=[seq(n_rows, KV_LORA), seq(n_rows, LANES), seq(s_len, KV_LORA), seq(s_len, B_ROPE), hbm, hbm],
        out_specs=pl.BlockSpec((None, n_rows, KV_LORA), lambda b, pt: (b, 0, 0)),
        scratch_shapes=[pltpu.VMEM((2, n_pages, PAGE_SIZE, KV_LORA), cache_ckv.dtype),
                        pltpu.VMEM((2, n_pages, B_ROPE, PAGE_SIZE), cache_kpe_t.dtype),
                        pltpu.SemaphoreType.DMA((2,)), pltpu.SemaphoreType.DMA((2,)),
                        pltpu.VMEM((n_pages * PAGE_SIZE, KV_LORA), BF16),
                        pltpu.VMEM((LANES, n_pages * PAGE_SIZE), BF16),
                        pltpu.VMEM((n_rows, 1), F32), pltpu.VMEM((n_rows, 1), F32),
                        pltpu.VMEM((n_rows, KV_LORA), F32)],
    )
    return pl.pallas_call(
        functools.partial(_attn_sample_kernel, n_pages=n_pages, n_groups=n_groups, n_split=n_split),
        grid_spec=grid_spec,
        out_shape=jax.ShapeDtypeStruct((dbs, n_rows, KV_LORA), BF16),
        compiler_params=_cparams("arbitrary"),
        name="mla_attn_sample",
    )(page_table, ql, qpe, c_new, pe_new, cache_ckv, cache_kpe_t)


def _rope_tables(pos):
    half = B_ROPE // 2
    inv = ROPE_BASE ** (-jnp.arange(half, dtype=F32) / half)
    ang = pos[:, None] * inv[None, :]
    cos, sin = jnp.cos(ang), jnp.sin(ang)
    zero = jnp.zeros_like(cos)
    return (jnp.concatenate([cos, cos, zero, zero], axis=1),
            jnp.concatenate([-sin, sin, zero, zero], axis=1))


def _dup_rope_cols(w):
    half = B_ROPE // 2
    x1, x2 = w[..., :half], w[..., half:]
    return jnp.concatenate([x1, x2, x2, x1], axis=-1)


def kernel(x_prompt, x_sample, state_mlstm_C, state_mlstm_n, state_mlstm_m, cache_ckv, cache_kpe, page_table,
           meta_tokens, a_w_in, a_b_gates, a_norm_g, a_w_out, a_ln_g, a_ln_b, kv_w_down, kv_norm_g, kv_w_up,
           b_w_in, b_q_norm_g, b_w_qb, b_w_out, b_ln_g, b_ln_b):
    bsz, seq, d_model = x_prompt.shape
    dbs, s_len, _ = x_sample.shape
    n_a, n_b = a_w_in.shape[0], b_w_in.shape[0]
    alpha = (2 * (n_a + n_b)) ** 0.25
    dqk = d_model // (2 * A_HEADS)
    dv = d_model // A_HEADS
    qk_w, v_w = A_HEADS * dqk, A_HEADS * dv
    main_w = 2 * qk_w + 3 * v_w
    t = seq + N_META
    n_chunks = seq // A_CHUNK
    past = page_table.shape[1] * PAGE_SIZE
    mp, ms = bsz * t, dbs * s_len

    n_tiles = 3 if t % (3 * BF16_SUBLANES) == 0 else 1
    bm_p = t // n_tiles
    bm_ln = _pick_block(mp, 384)
    bm_big = _pick_block(mp, 1376)
    bm_s = _pick_block(ms, 512)
    n_pages = min(32, page_table.shape[1])
    n_split = n_pages // 4 if n_pages % 4 == 0 else 1
    assert s_len <= SAMPLE_CHUNK and page_table.shape[1] % n_pages == 0

    w_up3 = kv_w_up.reshape(KV_LORA, B_HEADS, B_NOPE + B_VDIM)
    w_uk, w_uv = w_up3[..., :B_NOPE], w_up3[..., B_NOPE:]
    w_kv_heads = kv_w_up.astype(BF16)
    w_ukt = jnp.transpose(w_uk, (1, 2, 0)).astype(BF16)
    w_uvh = jnp.transpose(w_uv, (1, 0, 2)).astype(BF16)
    w_dn_ext = jnp.concatenate([kv_w_down[:, :KV_LORA], _dup_rope_cols(kv_w_down[:, KV_LORA:])], axis=1).astype(BF16)

    pos_p = jnp.arange(t, dtype=F32)
    pos_s = past + jnp.arange(s_len, dtype=F32)
    cs_p, sn_p = _rope_tables(pos_p)
    cs_s, sn_s = (jnp.tile(a, (bm_s // s_len, 1)) for a in _rope_tables(pos_s))

    meta = jnp.broadcast_to(meta_tokens.astype(x_prompt.dtype)[None], (bsz, N_META, d_model))
    xp = jnp.concatenate([meta, x_prompt], axis=1).reshape(mp, d_model)
    xs = x_sample.reshape(ms, d_model)
    xp16 = xs16 = None
    cache_kpe_t = jnp.swapaxes(cache_kpe, 1, 2)

    st_p, st_s = [], []
    for i in range(n_a):
        w_qkv = a_w_in[i][:, :2 * qk_w + v_w].astype(BF16)
        w_o = a_w_in[i][:, 2 * qk_w + v_w:2 * qk_w + 2 * v_w].astype(BF16)
        w_z = a_w_in[i][:, 2 * qk_w + 2 * v_w:main_w].astype(BF16)
        w_g = jnp.pad(a_w_in[i][:, main_w:], ((0, 0), (0, LANES - 2 * A_HEADS))).astype(BF16)
        b_g = jnp.pad(a_b_gates[i], (0, LANES - 2 * A_HEADS)).reshape(1, LANES)
        w_out = a_w_out[i].astype(BF16)

        xp_in = xp if xp16 is None else xp16
        xs_in = xs if xs16 is None else xs16
        u = _mm_plain(xp_in, w_qkv, BF16, bm=bm_p, bn=2048, name="a_in_prompt")
        gate_p = _mm_out_gate(xp_in, w_o, w_z, bm=bm_p, bn=1024, name="a_gate_prompt")
        gates = _mm_gates(xp_in, w_g, b_g, bm=bm_p, name="a_gates_prompt")
        g4 = gates[:, :2 * A_HEADS].reshape(bsz, t, 2, A_HEADS).transpose(0, 3, 2, 1)
        g_meta = g4[..., :N_META]
        g_chunks = g4[..., N_META:].reshape(bsz, A_HEADS, 2, n_chunks, A_CHUNK).transpose(0, 1, 3, 2, 4)
        h_p, c_p, n_p, m_p = _mlstm_prompt(u.reshape(bsz, t, -1), g_meta, g_chunks, dqk=dqk, dv=dv,
                                           heads=MLSTM_HEAD_GROUP)
        xp, xp16 = _mm_mlstm_out(h_p.reshape(mp, v_w), gate_p, a_norm_g[i], w_out, xp, a_ln_g[i], a_ln_b[i], alpha,
                                 (F32, BF16), heads=A_HEADS, bm=bm_ln, name="a_out_prompt")
        st_p.append((c_p, n_p.reshape(bsz, A_HEADS, dqk), m_p[:, :, 0, 0]))

        us = _mm_plain(xs_in, w_qkv, BF16, bm=bm_s, bn=1024, name="a_in_sample")
        gate_s = _mm_out_gate(xs_in, w_o, w_z, bm=bm_s, bn=1024, name="a_gate_sample")
        u_pad = jnp.pad(us.reshape(dbs, s_len, -1), ((0, 0), (0, SAMPLE_CHUNK - s_len), (0, 0)))
        gs = _mm_gates(xs_in, w_g, b_g, bm=bm_s, name="a_gates_sample")
        g3 = gs[:, :2 * A_HEADS].reshape(dbs, s_len, 2 * A_HEADS).transpose(0, 2, 1)
        tail = ((0, 0), (0, 0), (0, SAMPLE_CHUNK - s_len))
        g_rows = jnp.concatenate([jnp.pad(g3[:, :A_HEADS], tail, constant_values=-jnp.inf),
                                  jnp.pad(g3[:, A_HEADS:], tail)], axis=1)
        h_s, c_s, n_s, m_s = _mlstm_sample(u_pad, g_rows, state_mlstm_C[:, i], state_mlstm_n[:, i],
                                           state_mlstm_m[:, i][..., None], s_len=s_len, n_seq=1, dqk=dqk, dv=dv)
        xs, xs16 = _mm_mlstm_out(h_s.reshape(ms, v_w), gate_s, a_norm_g[i], w_out, xs, a_ln_g[i], a_ln_b[i], alpha,
                                 (F32, BF16), heads=A_HEADS, bm=bm_s, name="a_out_sample")
        st_s.append((c_s, n_s, m_s[..., 0]))

    if xp16 is None:
        xp16, xs16 = xp.astype(BF16), xs.astype(BF16)
    for j in range(n_b):
        outs = (F32,) if j == n_b - 1 else (F32, BF16)
        if j == 0:
            ckv_p, kpe_p, ckv16_p, kpe16_p = _mm_kv_down(xp16, w_dn_ext, kv_norm_g, cs_p, sn_p, bm=bm_p,
                                                         name="kv_down_prompt")
            ckv_s, kpe_s, _, _ = _mm_kv_down(xs16, w_dn_ext, kv_norm_g, cs_s, sn_s, bm=bm_s, name="kv_down_sample")
        w_ql = b_w_in[j][:, :Q_LORA].astype(BF16)
        w_z = b_w_in[j][:, Q_LORA:].astype(BF16)
        w_qb3 = b_w_qb[j].reshape(Q_LORA, B_HEADS, B_NOPE + B_ROPE)
        w_q_perm = jnp.concatenate([w_qb3[..., :B_NOPE], _dup_rope_cols(w_qb3[..., B_NOPE:])],
                                   axis=2).reshape(Q_LORA, -1).astype(BF16)
        w_out = b_w_out[j].astype(BF16)

        qlat = _mm_rms(xp16, w_ql, b_q_norm_g[j], bm=bm_big, name="b_qlat_prompt")
        z = _mm_plain(xp16, w_z, BF16, bm=bm_big, bn=1024, name="b_z_prompt")
        og = _attn_prompt(qlat.reshape(bsz, t, -1), w_q_perm, ckv16_p.reshape(bsz, t, -1), w_kv_heads,
                          kpe16_p.reshape(bsz, t, -1), cs_p, sn_p, z.reshape(bsz, t, -1), n_tiles=n_tiles)
        if j == n_b - 1:
            y_prompt = _mm_resid_ln_tail(og, w_out, xp.reshape(bsz, t, d_model), b_ln_g[j], b_ln_b[j], alpha,
                                         skip=N_META, bm=_pick_block(seq, 512), name="b_out_prompt")
        else:
            xp, xp16 = _mm_resid_ln(og.reshape(mp, -1), w_out, xp, b_ln_g[j], b_ln_b[j], alpha, outs, bm=bm_ln,
                                    name="b_out_prompt")

        qlat_s = _mm_rms(xs16, w_ql, b_q_norm_g[j], bm=bm_s, name="b_qlat_sample")
        z_s = _mm_plain(xs16, w_z, BF16, bm=bm_s, bn=1024, name="b_z_sample")
        qcat_s = _mm_q_up(qlat_s, w_q_perm, cs_s, sn_s, bm=bm_s, name="b_q_up_sample")
        ql = _mm_q_absorb(qcat_s, w_ukt, name="b_q_absorb_sample")
        qpe = qcat_s.reshape(ms, B_HEADS, QK_HEAD_W)[:, :, B_NOPE:]
        olat = _attn_sample(page_table, ql.reshape(dbs, s_len * B_HEADS, KV_LORA),
                            qpe.reshape(dbs, s_len * B_HEADS, LANES),
                            ckv_s.reshape(dbs, s_len, KV_LORA), kpe_s.reshape(dbs, s_len, B_ROPE),
                            cache_ckv, cache_kpe_t, n_pages=n_pages, n_split=n_split)
        ogs = _mm_o_up(olat.reshape(ms, B_HEADS * KV_LORA), w_uvh, z_s, name="b_o_up_sample")
        xs, *rest = _mm_resid_ln(ogs, w_out, xs, b_ln_g[j], b_ln_b[j], alpha, outs, bm=bm_s, name="b_out_sample")
        xs16 = rest[0] if rest else None

    if n_b == 0:
        y_prompt = xp.reshape(bsz, t, d_model)[:, N_META:]
    stack = lambda parts, k: jnp.stack([p[k] for p in parts], axis=1)
    return (y_prompt, xs.reshape(dbs, s_len, d_model),
            stack(st_p, 0), stack(st_p, 1), stack(st_p, 2),
            ckv_p.reshape(bsz, t, KV_LORA), kpe_p.reshape(bsz, t, B_ROPE),
            stack(st_s, 0), stack(st_s, 1), stack(st_s, 2),
            ckv_s.reshape(dbs, s_len, KV_LORA), kpe_s.reshape(dbs, s_len, B_ROPE))
```

```python
import functools

import jax
import jax.numpy as jnp
from jax import lax
from jax.experimental import pallas as pl
from jax.experimental.pallas import tpu as pltpu

F32 = jnp.float32
BF16 = jnp.bfloat16

N_META = 16
A_HEADS = 8
A_CHUNK = 64
IGATE_CAP = 15.0
B_HEADS = 16
B_NOPE = 128
B_ROPE = 64
B_VDIM = 128
KV_LORA = 512
Q_LORA = 512
PAGE_SIZE = 128
ROPE_BASE = 10000.0
LN_EPS = 1e-5
RMS_EPS = 1e-6
B_SCALE = (B_NOPE + B_ROPE) ** -0.5

LANES = 128
BF16_SUBLANES = 16
VMEM_LIMIT_BYTES = 56 * 1024 * 1024
HEAD_GROUP = 4
MLSTM_HEAD_GROUP = 4
SAMPLE_CHUNK = BF16_SUBLANES
QK_HEAD_W = 2 * LANES

NT_DIMS = (((1,), (1,)), ((), ()))
TN_DIMS = (((0,), (0,)), ((), ()))


def _cparams(*sem):
    return pltpu.CompilerParams(dimension_semantics=sem, vmem_limit_bytes=VMEM_LIMIT_BYTES)


def _pick_block(m, target, mult=BF16_SUBLANES):
    best = None
    for d in range(mult, min(m, target) + 1, mult):
        if m % d == 0:
            best = d
    return best if best is not None else m


def _layer_norm(y, g, b):
    mu = jnp.mean(y, axis=-1, keepdims=True)
    yc = y - mu
    var = jnp.mean(yc * yc, axis=-1, keepdims=True)
    return yc * lax.rsqrt(var + LN_EPS) * g + b


def _rms_norm(y, g):
    return y * lax.rsqrt(jnp.mean(y * y, axis=-1, keepdims=True) + RMS_EPS) * g


def _rope_dup(v, cs, sn):
    return v * cs + pltpu.roll(v, 2 * (B_ROPE // 2), axis=1) * sn


def _silu(z):
    return z * jax.nn.sigmoid(z)


def _mm_plain_kernel(x_ref, w_ref, o_ref):
    acc = jnp.dot(x_ref[...].astype(BF16), w_ref[...], preferred_element_type=F32)
    o_ref[...] = acc.astype(o_ref.dtype)


def _mm_plain(x, w, out_dtype, *, bm, bn, name):
    m, k = x.shape
    n = w.shape[1]
    return pl.pallas_call(
        _mm_plain_kernel,
        grid=(n // bn, m // bm),
        in_specs=[pl.BlockSpec((bm, k), lambda j, i: (i, 0)),
                  pl.BlockSpec((k, bn), lambda j, i: (0, j))],
        out_specs=pl.BlockSpec((bm, bn), lambda j, i: (i, j)),
        out_shape=jax.ShapeDtypeStruct((m, n), out_dtype),
        compiler_params=_cparams("arbitrary", "arbitrary"),
        name=name,
    )(x, w)


def _mm_out_gate_kernel(x_ref, wo_ref, wz_ref, g_ref):
    x = x_ref[...].astype(BF16)
    o = jnp.dot(x, wo_ref[...], preferred_element_type=F32)
    z = jnp.dot(x, wz_ref[...], preferred_element_type=F32)
    g_ref[...] = (jax.nn.sigmoid(o) * _silu(z)).astype(g_ref.dtype)


def _mm_out_gate(x, w_o, w_z, *, bm, bn, name):
    m, k = x.shape
    n = w_o.shape[1]
    wspec = pl.BlockSpec((k, bn), lambda j, i: (0, j))
    return pl.pallas_call(
        _mm_out_gate_kernel,
        grid=(n // bn, m // bm),
        in_specs=[pl.BlockSpec((bm, k), lambda j, i: (i, 0)), wspec, wspec],
        out_specs=pl.BlockSpec((bm, bn), lambda j, i: (i, j)),
        out_shape=jax.ShapeDtypeStruct((m, n), BF16),
        compiler_params=_cparams("arbitrary", "arbitrary"),
        name=name,
    )(x, w_o, w_z)


def _mm_gates_kernel(x_ref, w_ref, b_ref, o_ref):
    a = jnp.dot(x_ref[...].astype(BF16), w_ref[...], preferred_element_type=F32) + b_ref[...]
    lane = lax.broadcasted_iota(jnp.int32, a.shape, 1)
    gi = IGATE_CAP * jnp.tanh(a / IGATE_CAP)
    lf = jnp.minimum(a, 0.0) - jnp.log1p(jnp.exp(-jnp.abs(a)))
    o_ref[...] = jnp.where(lane < A_HEADS, gi, lf)


def _mm_gates(x, w, b, *, bm, name):
    m, k = x.shape
    return pl.pallas_call(
        _mm_gates_kernel,
        grid=(m // bm,),
        in_specs=[pl.BlockSpec((bm, k), lambda i: (i, 0)),
                  pl.BlockSpec((k, LANES), lambda i: (0, 0)),
                  pl.BlockSpec((1, LANES), lambda i: (0, 0))],
        out_specs=pl.BlockSpec((bm, LANES), lambda i: (i, 0)),
        out_shape=jax.ShapeDtypeStruct((m, LANES), F32),
        compiler_params=_cparams("arbitrary"),
        name=name,
    )(x, w, b)


def _mm_resid_ln_kernel(x_ref, w_ref, r_ref, g_ref, b_ref, *o_refs, alpha):
    acc = jnp.dot(x_ref[...].astype(BF16), w_ref[...], preferred_element_type=F32)
    y = _layer_norm(alpha * r_ref[...] + acc, g_ref[...], b_ref[...])
    for o_ref in o_refs:
        o_ref[...] = y.astype(o_ref.dtype)


def _mm_resid_ln(x, w, resid, g, b, alpha, out_dtypes, *, bm, name):
    m, k = x.shape
    n = w.shape[1]
    row = lambda i: (i, 0)
    fixed = lambda i: (0, 0)
    outs = pl.pallas_call(
        functools.partial(_mm_resid_ln_kernel, alpha=alpha),
        grid=(m // bm,),
        in_specs=[pl.BlockSpec((bm, k), row), pl.BlockSpec((k, n), fixed),
                  pl.BlockSpec((bm, n), row), pl.BlockSpec((1, n), fixed), pl.BlockSpec((1, n), fixed)],
        out_specs=[pl.BlockSpec((bm, n), row) for _ in out_dtypes],
        out_shape=[jax.ShapeDtypeStruct((m, n), dt) for dt in out_dtypes],
        compiler_params=_cparams("arbitrary"),
        name=name,
    )(x, w, resid, g.reshape(1, n), b.reshape(1, n))
    return outs


def _mm_mlstm_out_kernel(h_ref, gate_ref, ng_ref, w_ref, r_ref, g_ref, b_ref, *out_refs, alpha, heads):
    dv = h_ref.shape[1] // heads
    parts = []
    for hd in range(heads):
        cols = slice(hd * dv, (hd + 1) * dv)
        h = h_ref[:, cols].astype(F32)
        hn = h * lax.rsqrt(jnp.mean(h * h, axis=1, keepdims=True) + RMS_EPS) * ng_ref[:, cols]
        parts.append((hn * gate_ref[:, cols].astype(F32)).astype(BF16))
    acc = jnp.dot(jnp.concatenate(parts, axis=1), w_ref[...], preferred_element_type=F32)
    y = _layer_norm(alpha * r_ref[...] + acc, g_ref[...], b_ref[...])
    for out_ref in out_refs:
        out_ref[...] = y.astype(out_ref.dtype)


def _mm_mlstm_out(h, gate, norm_g, w, resid, g, b, alpha, out_dtypes, *, heads, bm, name):
    m, k = h.shape
    n = w.shape[1]
    row = lambda i: (i, 0)
    fixed = lambda i: (0, 0)
    return pl.pallas_call(
        functools.partial(_mm_mlstm_out_kernel, alpha=alpha, heads=heads),
        grid=(m // bm,),
        in_specs=[pl.BlockSpec((bm, k), row), pl.BlockSpec((bm, k), row), pl.BlockSpec((1, k), fixed),
                  pl.BlockSpec((k, n), fixed), pl.BlockSpec((bm, n), row),
                  pl.BlockSpec((1, n), fixed), pl.BlockSpec((1, n), fixed)],
        out_specs=[pl.BlockSpec((bm, n), row) for _ in out_dtypes],
        out_shape=[jax.ShapeDtypeStruct((m, n), dt) for dt in out_dtypes],
        compiler_params=_cparams("arbitrary"),
        name=name,
    )(h, gate, norm_g.reshape(1, k), w, resid, g.reshape(1, n), b.reshape(1, n))


def _mm_resid_ln_tail_kernel(x_ref, w_ref, r_ref, g_ref, b_ref, o_ref, *, alpha):
    acc = jnp.dot(x_ref[0].astype(BF16), w_ref[...], preferred_element_type=F32)
    o_ref[...] = _layer_norm(alpha * r_ref[0] + acc, g_ref[...], b_ref[...])


def _mm_resid_ln_tail(x, w, resid, g, b, alpha, *, skip, bm, name):
    bsz, t, k = x.shape
    n = w.shape[1]
    rows = t - skip
    tail = lambda width: pl.BlockSpec((pl.Element(1), pl.Element(bm), pl.Element(width)),
                                      lambda bi, i: (bi, pl.multiple_of(skip + i * bm, BF16_SUBLANES), 0))
    fixed = lambda bi, i: (0, 0)
    return pl.pallas_call(
        functools.partial(_mm_resid_ln_tail_kernel, alpha=alpha),
        grid=(bsz, rows // bm),
        in_specs=[tail(k), pl.BlockSpec((k, n), fixed), tail(n),
                  pl.BlockSpec((1, n), fixed), pl.BlockSpec((1, n), fixed)],
        out_specs=[pl.BlockSpec((None, bm, n), lambda bi, i: (bi, i, 0))],
        out_shape=[jax.ShapeDtypeStruct((bsz, rows, n), F32)],
        compiler_params=_cparams("arbitrary", "arbitrary"),
        name=name,
    )(x, w, resid, g.reshape(1, n), b.reshape(1, n))[0]


def _mm_kv_down_kernel(x_ref, w_ref, g_ref, cs_ref, sn_ref, ckv_ref, kpe_ref, ckv16_ref, kpe16_ref):
    acc = jnp.dot(x_ref[...].astype(BF16), w_ref[...], preferred_element_type=F32)
    ckv = _rms_norm(acc[:, :KV_LORA], g_ref[...])
    pe = _rope_dup(acc[:, KV_LORA:], cs_ref[...], sn_ref[...])
    ckv_ref[...] = ckv
    ckv16_ref[...] = ckv.astype(BF16)
    kpe_ref[...] = pe[:, :B_ROPE]
    kpe16_ref[...] = pe.astype(BF16)


def _mm_kv_down(x, w_ext, g, cs, sn, *, bm, name):
    m, k = x.shape
    n = w_ext.shape[1]
    nt = cs.shape[0] // bm
    row = lambda i: (i, 0)
    fixed = lambda i: (0, 0)
    tab = lambda i: (i % nt, 0)
    return pl.pallas_call(
        _mm_kv_down_kernel,
        grid=(m // bm,),
        in_specs=[pl.BlockSpec((bm, k), row), pl.BlockSpec((k, n), fixed), pl.BlockSpec((1, KV_LORA), fixed),
                  pl.BlockSpec((bm, LANES), tab), pl.BlockSpec((bm, LANES), tab)],
        out_specs=[pl.BlockSpec((bm, KV_LORA), row), pl.BlockSpec((bm, B_ROPE), row),
                   pl.BlockSpec((bm, KV_LORA), row), pl.BlockSpec((bm, LANES), row)],
        out_shape=[jax.ShapeDtypeStruct((m, KV_LORA), F32), jax.ShapeDtypeStruct((m, B_ROPE), F32),
                   jax.ShapeDtypeStruct((m, KV_LORA), BF16), jax.ShapeDtypeStruct((m, LANES), BF16)],
        compiler_params=_cparams("arbitrary"),
        name=name,
    )(x, w_ext, g.reshape(1, KV_LORA), cs, sn)


def _mm_rms_kernel(x_ref, w_ref, g_ref, o_ref):
    acc = jnp.dot(x_ref[...].astype(BF16), w_ref[...], preferred_element_type=F32)
    o_ref[...] = _rms_norm(acc, g_ref[...]).astype(o_ref.dtype)


def _mm_rms(x, w, g, *, bm, name):
    m, k = x.shape
    n = w.shape[1]
    return pl.pallas_call(
        _mm_rms_kernel,
        grid=(m // bm,),
        in_specs=[pl.BlockSpec((bm, k), lambda i: (i, 0)), pl.BlockSpec((k, n), lambda i: (0, 0)),
                  pl.BlockSpec((1, n), lambda i: (0, 0))],
        out_specs=pl.BlockSpec((bm, n), lambda i: (i, 0)),
        out_shape=jax.ShapeDtypeStruct((m, n), BF16),
        compiler_params=_cparams("arbitrary"),
        name=name,
    )(x, w, g.reshape(1, n))


def _mm_q_up_kernel(x_ref, w_ref, cs_ref, sn_ref, q_ref):
    acc = jnp.dot(x_ref[...], w_ref[...], preferred_element_type=F32) * B_SCALE
    cs = cs_ref[...]
    sn = sn_ref[...]
    for h in range(HEAD_GROUP):
        lo = h * QK_HEAD_W
        q_ref[:, lo:lo + B_NOPE] = acc[:, lo:lo + B_NOPE].astype(BF16)
        q_ref[:, lo + B_NOPE:lo + QK_HEAD_W] = _rope_dup(acc[:, lo + B_NOPE:lo + QK_HEAD_W], cs, sn).astype(BF16)


def _mm_q_up(qlat, w_perm, cs, sn, *, bm, name):
    m, k = qlat.shape
    ng = B_HEADS // HEAD_GROUP
    wn = HEAD_GROUP * QK_HEAD_W
    nt = cs.shape[0] // bm
    return pl.pallas_call(
        _mm_q_up_kernel,
        grid=(ng, m // bm),
        in_specs=[pl.BlockSpec((bm, k), lambda j, i: (i, 0)),
                  pl.BlockSpec((k, wn), lambda j, i: (0, j)),
                  pl.BlockSpec((bm, LANES), lambda j, i: (i % nt, 0)),
                  pl.BlockSpec((bm, LANES), lambda j, i: (i % nt, 0))],
        out_specs=pl.BlockSpec((bm, wn), lambda j, i: (i, j)),
        out_shape=jax.ShapeDtypeStruct((m, B_HEADS * QK_HEAD_W), BF16),
        compiler_params=_cparams("arbitrary", "arbitrary"),
        name=name,
    )(qlat, w_perm, cs, sn)


def _mm_heads_kernel(x_ref, w_ref, o_ref):
    o_ref[...] = jnp.dot(x_ref[...].astype(BF16), w_ref[...], preferred_element_type=F32).astype(o_ref.dtype)


def _mm_q_absorb(qcat, w_ukt, *, name):
    m = qcat.shape[0]
    return pl.pallas_call(
        _mm_heads_kernel,
        grid=(B_HEADS,),
        in_specs=[pl.BlockSpec((m, B_NOPE), lambda h: (0, 2 * h)),
                  pl.BlockSpec((None, B_NOPE, KV_LORA), lambda h: (h, 0, 0))],
        out_specs=pl.BlockSpec((m, KV_LORA), lambda h: (0, h)),
        out_shape=jax.ShapeDtypeStruct((m, B_HEADS * KV_LORA), BF16),
        compiler_params=_cparams("arbitrary"),
        name=name,
    )(qcat, w_ukt)


def _mm_o_up_kernel(x_ref, w_ref, z_ref, o_ref):
    acc = jnp.dot(x_ref[...].astype(BF16), w_ref[...], preferred_element_type=F32)
    o_ref[...] = (acc * _silu(z_ref[...].astype(F32))).astype(o_ref.dtype)


def _mm_o_up(olat, w_uv, z, *, name):
    m = olat.shape[0]
    return pl.pallas_call(
        _mm_o_up_kernel,
        grid=(B_HEADS,),
        in_specs=[pl.BlockSpec((m, KV_LORA), lambda h: (0, h)),
                  pl.BlockSpec((None, KV_LORA, B_VDIM), lambda h: (h, 0, 0)),
                  pl.BlockSpec((m, B_VDIM), lambda h: (0, h))],
        out_specs=pl.BlockSpec((m, B_VDIM), lambda h: (0, h)),
        out_shape=jax.ShapeDtypeStruct((m, B_HEADS * B_VDIM), BF16),
        compiler_params=_cparams("arbitrary"),
        name=name,
    )(olat, w_uv, z)


def _mlstm_gate_terms(lf_row, gi_row, m_prev):
    length = lf_row.shape[1]
    t_idx = lax.broadcasted_iota(jnp.int32, (length, length), 0)
    s_idx = lax.broadcasted_iota(jnp.int32, (length, length), 1)
    causal = s_idx <= t_idx
    eye = s_idx == t_idx
    b_col = jnp.sum(jnp.where(causal, lf_row, 0.0), axis=1, keepdims=True)
    b_row = jnp.sum(jnp.where(eye, b_col, 0.0), axis=0, keepdims=True)
    gi_col = jnp.sum(jnp.where(eye, gi_row, 0.0), axis=1, keepdims=True)
    log_d = jnp.where(causal, b_col - b_row + gi_row, -jnp.inf)
    m_t = jnp.maximum(jnp.max(log_d, axis=1, keepdims=True), b_col + m_prev)
    w_inter = jnp.exp(b_col + m_prev - m_t)
    d = jnp.exp(log_d - m_t)
    b_end = b_row[:, length - 1:length]
    log_w_row = b_end - b_row + gi_row
    m_new = jnp.maximum(b_end + m_prev, jnp.max(log_w_row, axis=1, keepdims=True))
    w_c = jnp.exp(b_end + m_prev - m_new)
    w_s_col = jnp.exp(b_end - b_col + gi_col - m_new)
    return d, m_t, w_inter, w_s_col, w_c, m_new


def _mlstm_chunk_heads(heads, scale):
    stage_a = []
    for q, k, v, gi_row, lf_row, c_prev, n_prev, m_prev in heads:
        terms = _mlstm_gate_terms(lf_row, gi_row, m_prev)
        qk = lax.dot_general(q, k, NT_DIMS, preferred_element_type=F32)
        q_c = jnp.dot(q, c_prev.astype(BF16), preferred_element_type=F32)
        stage_a.append((terms, qk, q_c))
    stage_b = []
    for (q, k, v, _, _, c_prev, n_prev, _), (terms, qk, q_c) in zip(heads, stage_a):
        d, m_t, w_inter, w_s_col, w_c, m_new = terms
        s = qk * (scale * d)
        s_v = jnp.dot(s.astype(BF16), v, preferred_element_type=F32)
        w_v = (w_s_col * v.astype(F32)).astype(BF16)
        k_v = lax.dot_general(k, w_v, TN_DIMS, preferred_element_type=F32)
        stage_b.append((s, s_v, k_v))
    results = []
    for (q, k, v, _, _, c_prev, n_prev, _), (terms, _, q_c), (s, s_v, k_v) in zip(heads, stage_a, stage_b):
        d, m_t, w_inter, w_s_col, w_c, m_new = terms
        w_q = w_inter * scale
        num = w_q * q_c + s_v
        q_n = jnp.sum(q.astype(F32) * n_prev, axis=1, keepdims=True)
        den = w_q * q_n + jnp.sum(s, axis=1, keepdims=True)
        h = num / jnp.maximum(jnp.abs(den), jnp.exp(-m_t))
        c_new = w_c * c_prev + k_v
        n_new = w_c * n_prev + jnp.sum(w_s_col * k.astype(F32), axis=0, keepdims=True)
        results.append((h, c_new, n_new, m_new))
    return results


def _mlstm_prompt_kernel(q_ref, k_ref, v_ref, gm_ref, gc_ref, h_ref, c_ref, n_ref, m_out_ref, *, n_chunks, heads):
    dqk = q_ref.shape[1] // heads
    dv = v_ref.shape[1] // heads
    scale = dqk ** -0.5

    def run_chunk(rows, gates, m_prev):
        inputs = []
        for h in range(heads):
            qk_cols = slice(h * dqk, (h + 1) * dqk)
            v_cols = slice(h * dv, (h + 1) * dv)
            inputs.append((q_ref[rows, qk_cols], k_ref[rows, qk_cols], v_ref[rows, v_cols], *gates(h),
                           c_ref[h], n_ref[h], m_prev[h]))
        m_next = []
        for h, (out, c_new, n_new, m_new) in enumerate(_mlstm_chunk_heads(inputs, scale)):
            h_ref[rows, h * dv:(h + 1) * dv] = out.astype(h_ref.dtype)
            c_ref[h] = c_new
            n_ref[h] = n_new
            m_next.append(m_new)
        return tuple(m_next)

    c_ref[...] = jnp.zeros(c_ref.shape, F32)
    n_ref[...] = jnp.zeros(n_ref.shape, F32)
    m_state = tuple(jnp.zeros((1, 1), F32) for _ in range(heads))
    m_state = run_chunk(pl.ds(0, N_META), lambda h: (gm_ref[h, 0:1, :], gm_ref[h, 1:2, :]), m_state)

    def body(c, m_prev):
        r0 = pl.multiple_of(N_META + c * A_CHUNK, BF16_SUBLANES)
        return run_chunk(pl.ds(r0, A_CHUNK), lambda h: (gc_ref[h, c, 0:1, :], gc_ref[h, c, 1:2, :]), m_prev)

    m_state = lax.fori_loop(0, n_chunks, body, m_state)
    for h in range(heads):
        m_out_ref[h] = jnp.broadcast_to(m_state[h], m_out_ref.shape[1:])


def _mlstm_prompt(u, g_meta, g_chunks, *, dqk, dv, heads):
    bsz, t, _ = u.shape
    n_chunks = g_chunks.shape[2]
    groups = A_HEADS // heads
    v_base = 2 * A_HEADS * dqk // (heads * dv)
    tok = lambda width, base: pl.BlockSpec((None, t, heads * width), lambda b, g: (b, 0, base + g))
    hsp = lambda *shape: pl.BlockSpec((None, heads) + shape, lambda b, g: (b, g) + (0,) * len(shape))
    return pl.pallas_call(
        functools.partial(_mlstm_prompt_kernel, n_chunks=n_chunks, heads=heads),
        grid=(bsz, groups),
        in_specs=[tok(dqk, 0), tok(dqk, groups), tok(dv, v_base), hsp(2, N_META), hsp(n_chunks, 2, A_CHUNK)],
        out_specs=[pl.BlockSpec((None, t, heads * dv), lambda b, g: (b, 0, g)),
                   hsp(dqk, dv), hsp(1, dqk), hsp(1, LANES)],
        out_shape=[jax.ShapeDtypeStruct((bsz, t, A_HEADS * dv), BF16),
                   jax.ShapeDtypeStruct((bsz, A_HEADS, dqk, dv), F32),
                   jax.ShapeDtypeStruct((bsz, A_HEADS, 1, dqk), F32),
                   jax.ShapeDtypeStruct((bsz, A_HEADS, 1, LANES), F32)],
        compiler_params=_cparams("arbitrary", "arbitrary"),
        name="mlstm_prompt",
    )(u, u, u, g_meta, g_chunks)


def _mlstm_sample_kernel(u_ref, g_ref, c_ref, n_ref, m_ref, h_ref, c_out_ref, n_out_ref, m_out_ref, *, dqk, dv):
    n_seq, s_len = h_ref.shape[0], h_ref.shape[1]
    scale = dqk ** -0.5
    k_base = A_HEADS * dqk
    v_base = 2 * A_HEADS * dqk
    inputs = []
    for i in range(n_seq):
        for h in range(A_HEADS):
            inputs.append((
                u_ref[i, :, h * dqk:(h + 1) * dqk], u_ref[i, :, k_base + h * dqk:k_base + (h + 1) * dqk],
                u_ref[i, :, v_base + h * dv:v_base + (h + 1) * dv],
                g_ref[i, h:h + 1, :], g_ref[i, A_HEADS + h:A_HEADS + h + 1, :],
                c_ref[i, h], n_ref[i, h:h + 1, :], m_ref[i, h:h + 1, :]))
    for idx, (out, c_new, n_new, m_new) in enumerate(_mlstm_chunk_heads(inputs, scale)):
        i, h = divmod(idx, A_HEADS)
        h_ref[i, :, h * dv:(h + 1) * dv] = out[:s_len]
        c_out_ref[i, h] = c_new
        n_out_ref[i, h:h + 1, :] = n_new
        m_out_ref[i, h:h + 1, :] = m_new


def _mlstm_sample(u_pad, g_rows, c_prev, n_prev, m_prev, *, s_len, n_seq, dqk, dv):
    dbs, length, width = u_pad.shape
    seq3 = lambda *shape: pl.BlockSpec((n_seq,) + shape, lambda b: (b,) + (0,) * len(shape))
    return pl.pallas_call(
        functools.partial(_mlstm_sample_kernel, dqk=dqk, dv=dv),
        grid=(dbs // n_seq,),
        in_specs=[seq3(length, width), seq3(2 * A_HEADS, length),
                  seq3(A_HEADS, dqk, dv), seq3(A_HEADS, dqk), seq3(A_HEADS, 1)],
        out_specs=[seq3(s_len, A_HEADS * dv), seq3(A_HEADS, dqk, dv), seq3(A_HEADS, dqk), seq3(A_HEADS, 1)],
        out_shape=[jax.ShapeDtypeStruct((dbs, s_len, A_HEADS * dv), F32),
                   jax.ShapeDtypeStruct((dbs, A_HEADS, dqk, dv), F32),
                   jax.ShapeDtypeStruct((dbs, A_HEADS, dqk), F32),
                   jax.ShapeDtypeStruct((dbs, A_HEADS, 1), F32)],
        compiler_params=_cparams("arbitrary"),
        name="mlstm_sample",
    )(u_pad, g_rows, c_prev, n_prev, m_prev)


def _attn_prompt_kernel(qlat_ref, wq_ref, ckv_ref, wkv_ref, kpe_ref, cs_ref, sn_ref, z_ref, o_ref,
                        q_buf, k_buf, v_buf, *, n_tiles):
    kv = jnp.dot(ckv_ref[...], wkv_ref[...], preferred_element_type=F32)
    k_buf[:, :B_NOPE] = kv[:, :B_NOPE].astype(BF16)
    k_buf[:, B_NOPE:] = kpe_ref[...]
    v_buf[...] = kv[:, B_NOPE:].astype(BF16)
    qf = jnp.dot(qlat_ref[...], wq_ref[...], preferred_element_type=F32) * B_SCALE
    q_buf[:, :B_NOPE] = qf[:, :B_NOPE].astype(BF16)
    q_buf[:, B_NOPE:] = _rope_dup(qf[:, B_NOPE:], cs_ref[...], sn_ref[...]).astype(BF16)

    tq = q_buf.shape[0] // n_tiles
    row = lax.broadcasted_iota(jnp.int32, (tq, tq), 0)
    col = lax.broadcasted_iota(jnp.int32, (tq, tq), 1)
    for i in range(n_tiles):
        rows = pl.ds(i * tq, tq)
        q = q_buf[rows, :]
        m = jnp.full((tq, 1), -jnp.inf, F32)
        l = jnp.zeros((tq, 1), F32)
        acc = jnp.zeros((tq, v_buf.shape[1]), F32)
        for j in range(i + 1):
            keys = pl.ds(j * tq, tq)
            s = lax.dot_general(q, k_buf[keys, :], NT_DIMS, preferred_element_type=F32)
            if j == i:
                s = jnp.where(col <= row, s, -jnp.inf)
            m_new = jnp.maximum(m, jnp.max(s, axis=1, keepdims=True))
            alpha = jnp.exp(m - m_new)
            p = jnp.exp(s - m_new)
            l = alpha * l + jnp.sum(p, axis=1, keepdims=True)
            acc = alpha * acc + jnp.dot(p.astype(BF16), v_buf[keys, :], preferred_element_type=F32)
            m = m_new
        o_ref[rows, :] = ((acc / l) * _silu(z_ref[rows, :].astype(F32))).astype(o_ref.dtype)


def _attn_prompt(qlat, w_q, ckv16, w_kv, kpe16, cs, sn, z, *, n_tiles):
    bsz, t, _ = qlat.shape
    per_b = lambda width: pl.BlockSpec((None, t, width), lambda b, h: (b, 0, 0))
    per_h = lambda rows, width: pl.BlockSpec((rows, width), lambda b, h: (0, h))
    tab = pl.BlockSpec((t, LANES), lambda b, h: (0, 0))
    hv = pl.BlockSpec((None, t, B_VDIM), lambda b, h: (b, 0, h))
    return pl.pallas_call(
        functools.partial(_attn_prompt_kernel, n_tiles=n_tiles),
        grid=(bsz, B_HEADS),
        in_specs=[per_b(Q_LORA), per_h(Q_LORA, QK_HEAD_W), per_b(KV_LORA), per_h(KV_LORA, B_NOPE + B_VDIM),
                  per_b(LANES), tab, tab, hv],
        out_specs=hv,
        out_shape=jax.ShapeDtypeStruct((bsz, t, B_HEADS * B_VDIM), BF16),
        scratch_shapes=[pltpu.VMEM((t, QK_HEAD_W), BF16), pltpu.VMEM((t, QK_HEAD_W), BF16),
                        pltpu.VMEM((t, B_VDIM), BF16)],
        compiler_params=_cparams("arbitrary", "arbitrary"),
        name="mla_attn_prompt",
    )(qlat, w_q, ckv16, w_kv, kpe16, cs, sn, z)


def _attn_sample_kernel(pt_ref, ql_ref, qpe_ref, cn_ref, pn_ref, ckv_hbm, kpe_hbm, o_ref,
                        cpage, ppage, sem_c, sem_p, cbuf, pbuf, m_sc, l_sc, acc_sc, *, n_pages, n_groups, n_split):
    b = pl.program_id(0)
    last_seq = pl.num_programs(0) - 1
    per = n_pages // n_split
    width = per * PAGE_SIZE

    def page_copies(seq, grp, slot):
        copies = []
        for g in range(n_pages):
            page = pt_ref[seq, grp * n_pages + g]
            copies.append(pltpu.make_async_copy(ckv_hbm.at[page], cpage.at[slot, g], sem_c.at[slot]))
            copies.append(pltpu.make_async_copy(kpe_hbm.at[page], ppage.at[slot, g], sem_p.at[slot]))
        return copies

    def start_all(copies):
        for idx, copy in enumerate(copies):
            copy.start(priority=(idx // 2) % 2)

    @pl.when(b == 0)
    def _():
        start_all(page_copies(0, 0, 0))

    m_sc[...] = jnp.full(m_sc.shape, -jnp.inf, F32)
    l_sc[...] = jnp.zeros(l_sc.shape, F32)
    acc_sc[...] = jnp.zeros(acc_sc.shape, F32)
    pbuf[B_ROPE:, :] = jnp.zeros((pbuf.shape[0] - B_ROPE, pbuf.shape[1]), BF16)
    ql = ql_ref[...]
    qpe = qpe_ref[...]

    def group(k, carry):
        slot = lax.rem(b * n_groups + k, 2)
        last_grp = k == n_groups - 1

        @pl.when(jnp.logical_not(jnp.logical_and(last_grp, b == last_seq)))
        def _():
            nxt_seq = jnp.where(last_grp, b + 1, b)
            nxt_grp = jnp.where(last_grp, 0, k + 1)
            start_all(page_copies(nxt_seq, nxt_grp, 1 - slot))

        for copy in page_copies(b, k, slot):
            copy.wait()

        scores = []
        for part in range(n_split):
            for g in range(part * per, (part + 1) * per):
                keys = pl.ds(g * PAGE_SIZE, PAGE_SIZE)
                cbuf[keys, :] = cpage[slot, g].astype(BF16)
                pbuf[0:B_ROPE, keys] = ppage[slot, g].astype(BF16)
            keys = pl.ds(part * width, width)
            scores.append(lax.dot_general(ql, cbuf[keys, :], NT_DIMS, preferred_element_type=F32)
                          + jnp.dot(qpe, pbuf[:, keys], preferred_element_type=F32))
        partials = []
        for part, s in enumerate(scores):
            m_part = jnp.max(s, axis=1, keepdims=True)
            p = jnp.exp(s - m_part)
            partials.append((m_part, jnp.sum(p, axis=1, keepdims=True),
                             jnp.dot(p.astype(BF16), cbuf[pl.ds(part * width, width), :],
                                     preferred_element_type=F32)))

        m_prev = m_sc[...]
        m_new = m_prev
        for m_part, _, _ in partials:
            m_new = jnp.maximum(m_new, m_part)
        w_prev = jnp.exp(m_prev - m_new)
        l_new = w_prev * l_sc[...]
        acc = w_prev * acc_sc[...]
        for m_part, l_part, a_part in partials:
            w_part = jnp.exp(m_part - m_new)
            l_new = l_new + w_part * l_part
            acc = acc + w_part * a_part
        m_sc[...] = m_new
        l_sc[...] = l_new
        acc_sc[...] = acc
        return carry

    lax.fori_loop(0, n_groups, group, 0)

    n_rows = ql.shape[0]
    s_len = cn_ref.shape[0]
    tok = lax.broadcasted_iota(jnp.int32, (n_rows, 1), 0) // B_HEADS
    qlf = ql.astype(F32)
    qpf = qpe.astype(F32)[:, :B_ROPE]
    m_run, l_run, acc_run = m_sc[...], l_sc[...], acc_sc[...]
    for t in range(s_len):
        c_row = cn_ref[t:t + 1, :]
        sc = (jnp.sum(qlf * c_row, axis=1, keepdims=True)
              + jnp.sum(qpf * pn_ref[t:t + 1, :], axis=1, keepdims=True))
        sc = jnp.where(tok >= t, sc, -jnp.inf)
        m_nxt = jnp.maximum(m_run, sc)
        a = jnp.exp(m_run - m_nxt)
        pt = jnp.exp(sc - m_nxt)
        l_run = a * l_run + pt
        acc_run = a * acc_run + pt * c_row
        m_run = m_nxt
    o_ref[...] = (acc_run / l_run).astype(o_ref.dtype)


def _attn_sample(page_table, ql, qpe, c_new, pe_new, cache_ckv, cache_kpe_t, *, n_pages, n_split):
    dbs, n_rows, _ = ql.shape
    s_len = c_new.shape[1]
    n_groups = page_table.shape[1] // n_pages
    seq = lambda *shape: pl.BlockSpec((None,) + shape, lambda b, pt: (b,) + (0,) * len(shape))
    hbm = pl.BlockSpec(memory_space=pl.ANY)
    grid_spec = pltpu.PrefetchScalarGridSpec(
        num_scalar_prefetch=1,
        grid=(dbs,),
        in_specs=[seq(n_rows, KV_LORA), seq(n_rows, LANES), seq(s_len, KV_LORA), seq(s_len, B_ROPE), hbm, hbm],
        out_specs=pl.BlockSpec((None, n_rows, KV_LORA), lambda b, pt: (b, 0, 0)),
        scratch_shapes=[pltpu.VMEM((2, n_pages, PAGE_SIZE, KV_LORA), cache_ckv.dtype),
                        pltpu.VMEM((2, n_pages, B_ROPE, PAGE_SIZE), cache_kpe_t.dtype),
                        pltpu.SemaphoreType.DMA((2,)), pltpu.SemaphoreType.DMA((2,)),
                        pltpu.VMEM((n_pages * PAGE_SIZE, KV_LORA), BF16),
                        pltpu.VMEM((LANES, n_pages * PAGE_SIZE), BF16),
                        pltpu.VMEM((n_rows, 1), F32), pltpu.VMEM((n_rows, 1), F32),
                        pltpu.VMEM((n_rows, KV_LORA), F32)],
    )
    return pl.pallas_call(
        functools.partial(_attn_sample_kernel, n_pages=n_pages, n_groups=n_groups, n_split=n_split),
        grid_spec=grid_spec,
        out_shape=jax.ShapeDtypeStruct((dbs, n_rows, KV_LORA), BF16),
        compiler_params=_cparams("arbitrary"),
        name="mla_attn_sample",
    )(page_table, ql, qpe, c_new, pe_new, cache_ckv, cache_kpe_t)


def _rope_tables(pos):
    half = B_ROPE // 2
    inv = ROPE_BASE ** (-jnp.arange(half, dtype=F32) / half)
    ang = pos[:, None] * inv[None, :]
    cos, sin = jnp.cos(ang), jnp.sin(ang)
    zero = jnp.zeros_like(cos)
    return (jnp.concatenate([cos, cos, zero, zero], axis=1),
            jnp.concatenate([-sin, sin, zero, zero], axis=1))


def _dup_rope_cols(w):
    half = B_ROPE // 2
    x1, x2 = w[..., :half], w[..., half:]
    return jnp.concatenate([x1, x2, x2, x1], axis=-1)


def kernel(x_prompt, x_sample, state_mlstm_C, state_mlstm_n, state_mlstm_m, cache_ckv, cache_kpe, page_table,
           meta_tokens, a_w_in, a_b_gates, a_norm_g, a_w_out, a_ln_g, a_ln_b, kv_w_down, kv_norm_g, kv_w_up,
           b_w_in, b_q_norm_g, b_w_qb, b_w_out, b_ln_g, b_ln_b):
    bsz, seq, d_model = x_prompt.shape
    dbs, s_len, _ = x_sample.shape
    n_a, n_b = a_w_in.shape[0], b_w_in.shape[0]
    alpha = (2 * (n_a + n_b)) ** 0.25
    dqk = d_model // (2 * A_HEADS)
    dv = d_model // A_HEADS
    qk_w, v_w = A_HEADS * dqk, A_HEADS * dv
    main_w = 2 * qk_w + 3 * v_w
    t = seq + N_META
    n_chunks = seq // A_CHUNK
    past = page_table.shape[1] * PAGE_SIZE
    mp, ms = bsz * t, dbs * s_len

    n_tiles = 3 if t % (3 * BF16_SUBLANES) == 0 else 1
    bm_p = t // n_tiles
    bm_ln = _pick_block(mp, 384)
    bm_big = _pick_block(mp, 1376)
    bm_s = _pick_block(ms, 512)
    n_pages = min(32, page_table.shape[1])
    n_split = n_pages // 4 if n_pages % 4 == 0 else 1
    assert s_len <= SAMPLE_CHUNK and page_table.shape[1] % n_pages == 0

    w_up3 = kv_w_up.reshape(KV_LORA, B_HEADS, B_NOPE + B_VDIM)
    w_uk, w_uv = w_up3[..., :B_NOPE], w_up3[..., B_NOPE:]
    w_kv_heads = kv_w_up.astype(BF16)
    w_ukt = jnp.transpose(w_uk, (1, 2, 0)).astype(BF16)
    w_uvh = jnp.transpose(w_uv, (1, 0, 2)).astype(BF16)
    w_dn_ext = jnp.concatenate([kv_w_down[:, :KV_LORA], _dup_rope_cols(kv_w_down[:, KV_LORA:])], axis=1).astype(BF16)

    pos_p = jnp.arange(t, dtype=F32)
    pos_s = past + jnp.arange(s_len, dtype=F32)
    cs_p, sn_p = _rope_tables(pos_p)
    cs_s, sn_s = (jnp.tile(a, (bm_s // s_len, 1)) for a in _rope_tables(pos_s))

    meta = jnp.broadcast_to(meta_tokens.astype(x_prompt.dtype)[None], (bsz, N_META, d_model))
    xp = jnp.concatenate([meta, x_prompt], axis=1).reshape(mp, d_model)
    xs = x_sample.reshape(ms, d_model)
    xp16 = xs16 = None
    cache_kpe_t = jnp.swapaxes(cache_kpe, 1, 2)

    st_p, st_s = [], []
    for i in range(n_a):
        w_qkv = a_w_in[i][:, :2 * qk_w + v_w].astype(BF16)
        w_o = a_w_in[i][:, 2 * qk_w + v_w:2 * qk_w + 2 * v_w].astype(BF16)
        w_z = a_w_in[i][:, 2 * qk_w + 2 * v_w:main_w].astype(BF16)
        w_g = jnp.pad(a_w_in[i][:, main_w:], ((0, 0), (0, LANES - 2 * A_HEADS))).astype(BF16)
        b_g = jnp.pad(a_b_gates[i], (0, LANES - 2 * A_HEADS)).reshape(1, LANES)
        w_out = a_w_out[i].astype(BF16)

        xp_in = xp if xp16 is None else xp16
        xs_in = xs if xs16 is None else xs16
        u = _mm_plain(xp_in, w_qkv, BF16, bm=bm_p, bn=2048, name="a_in_prompt")
        gate_p = _mm_out_gate(xp_in, w_o, w_z, bm=bm_p, bn=1024, name="a_gate_prompt")
        gates = _mm_gates(xp_in, w_g, b_g, bm=bm_p, name="a_gates_prompt")
        g4 = gates[:, :2 * A_HEADS].reshape(bsz, t, 2, A_HEADS).transpose(0, 3, 2, 1)
        g_meta = g4[..., :N_META]
        g_chunks = g4[..., N_META:].reshape(bsz, A_HEADS, 2, n_chunks, A_CHUNK).transpose(0, 1, 3, 2, 4)
        h_p, c_p, n_p, m_p = _mlstm_prompt(u.reshape(bsz, t, -1), g_meta, g_chunks, dqk=dqk, dv=dv,
                                           heads=MLSTM_HEAD_GROUP)
        xp, xp16 = _mm_mlstm_out(h_p.reshape(mp, v_w), gate_p, a_norm_g[i], w_out, xp, a_ln_g[i], a_ln_b[i], alpha,
                                 (F32, BF16), heads=A_HEADS, bm=bm_ln, name="a_out_prompt")
        st_p.append((c_p, n_p.reshape(bsz, A_HEADS, dqk), m_p[:, :, 0, 0]))

        us = _mm_plain(xs_in, w_qkv, BF16, bm=bm_s, bn=1024, name="a_in_sample")
        gate_s = _mm_out_gate(xs_in, w_o, w_z, bm=bm_s, bn=1024, name="a_gate_sample")
        u_pad = jnp.pad(us.reshape(dbs, s_len, -1), ((0, 0), (0, SAMPLE_CHUNK - s_len), (0, 0)))
        gs = _mm_gates(xs_in, w_g, b_g, bm=bm_s, name="a_gates_sample")
        g3 = gs[:, :2 * A_HEADS].reshape(dbs, s_len, 2 * A_HEADS).transpose(0, 2, 1)
        tail = ((0, 0), (0, 0), (0, SAMPLE_CHUNK - s_len))
        g_rows = jnp.concatenate([jnp.pad(g3[:, :A_HEADS], tail, constant_values=-jnp.inf),
                                  jnp.pad(g3[:, A_HEADS:], tail)], axis=1)
        h_s, c_s, n_s, m_s = _mlstm_sample(u_pad, g_rows, state_mlstm_C[:, i], state_mlstm_n[:, i],
                                           state_mlstm_m[:, i][..., None], s_len=s_len, n_seq=1, dqk=dqk, dv=dv)
        xs, xs16 = _mm_mlstm_out(h_s.reshape(ms, v_w), gate_s, a_norm_g[i], w_out, xs, a_ln_g[i], a_ln_b[i], alpha,
                                 (F32, BF16), heads=A_HEADS, bm=bm_s, name="a_out_sample")
        st_s.append((c_s, n_s, m_s[..., 0]))

    if xp16 is None:
        xp16, xs16 = xp.astype(BF16), xs.astype(BF16)
    for j in range(n_b):
        outs = (F32,) if j == n_b - 1 else (F32, BF16)
        if j == 0:
            ckv_p, kpe_p, ckv16_p, kpe16_p = _mm_kv_down(xp16, w_dn_ext, kv_norm_g, cs_p, sn_p, bm=bm_p,
                                                         name="kv_down_prompt")
            ckv_s, kpe_s, _, _ = _mm_kv_down(xs16, w_dn_ext, kv_norm_g, cs_s, sn_s, bm=bm_s, name="kv_down_sample")
        w_ql = b_w_in[j][:, :Q_LORA].astype(BF16)
        w_z = b_w_in[j][:, Q_LORA:].astype(BF16)
        w_qb3 = b_w_qb[j].reshape(Q_LORA, B_HEADS, B_NOPE + B_ROPE)
        w_q_perm = jnp.concatenate([w_qb3[..., :B_NOPE], _dup_rope_cols(w_qb3[..., B_NOPE:])],
                                   axis=2).reshape(Q_LORA, -1).astype(BF16)
        w_out = b_w_out[j].astype(BF16)

        qlat = _mm_rms(xp16, w_ql, b_q_norm_g[j], bm=bm_big, name="b_qlat_prompt")
        z = _mm_plain(xp16, w_z, BF16, bm=bm_big, bn=1024, name="b_z_prompt")
        og = _attn_prompt(qlat.reshape(bsz, t, -1), w_q_perm, ckv16_p.reshape(bsz, t, -1), w_kv_heads,
                          kpe16_p.reshape(bsz, t, -1), cs_p, sn_p, z.reshape(bsz, t, -1), n_tiles=n_tiles)
        if j == n_b - 1:
            y_prompt = _mm_resid_ln_tail(og, w_out, xp.reshape(bsz, t, d_model), b_ln_g[j], b_ln_b[j], alpha,
                                         skip=N_META, bm=_pick_block(seq, 512), name="b_out_prompt")
        else:
            xp, xp16 = _mm_resid_ln(og.reshape(mp, -1), w_out, xp, b_ln_g[j], b_ln_b[j], alpha, outs, bm=bm_ln,
                                    name="b_out_prompt")

        qlat_s = _mm_rms(xs16, w_ql, b_q_norm_g[j], bm=bm_s, name="b_qlat_sample")
        z_s = _mm_plain(xs16, w_z, BF16, bm=bm_s, bn=1024, name="b_z_sample")
        qcat_s = _mm_q_up(qlat_s, w_q_perm, cs_s, sn_s, bm=bm_s, name="b_q_up_sample")
        ql = _mm_q_absorb(qcat_s, w_ukt, name="b_q_absorb_sample")
        qpe = qcat_s.reshape(ms, B_HEADS, QK_HEAD_W)[:, :, B_NOPE:]
        olat = _attn_sample(page_table, ql.reshape(dbs, s_len * B_HEADS, KV_LORA),
                            qpe.reshape(dbs, s_len * B_HEADS, LANES),
                            ckv_s.reshape(dbs, s_len, KV_LORA), kpe_s.reshape(dbs, s_len, B_ROPE),
                            cache_ckv, cache_kpe_t, n_pages=n_pages, n_split=n_split)
        ogs = _mm_o_up(olat.reshape(ms, B_HEADS * KV_LORA), w_uvh, z_s, name="b_o_up_sample")
        xs, *rest = _mm_resid_ln(ogs, w_out, xs, b_ln_g[j], b_ln_b[j], alpha, outs, bm=bm_s, name="b_out_sample")
        xs16 = rest[0] if rest else None

    if n_b == 0:
        y_prompt = xp.reshape(bsz, t, d_model)[:, N_META:]
    stack = lambda parts, k: jnp.stack([p[k] for p in parts], axis=1)
    return (y_prompt, xs.reshape(dbs, s_len, d_model),
            stack(st_p, 0), stack(st_p, 1), stack(st_p, 2),
            ckv_p.reshape(bsz, t, KV_LORA), kpe_p.reshape(bsz, t, B_ROPE),
            stack(st_s, 0), stack(st_s, 1), stack(st_s, 2),
            ckv_s.reshape(dbs, s_len, KV_LORA), kpe_s.reshape(dbs, s_len, B_ROPE))
```
